```python
import jax
import jax.numpy as jnp
from jax import lax
import numpy as np

D_MODEL = 2048
BATCH = 8
SEQ = 2048
DEPTH = 2
DEC_BATCH = 32
DEC_SEQ = 64
PAST_LEN = 4096

CHUNK = 64
N_MEM = 256
SB_HEADS = 8
SB_HEAD_DIM = 128
SB_WIDTH = SB_HEADS * SB_HEAD_DIM
SB_BLOCK = 128
GLA_HEADS = 4
GLA_DK = 128
GLA_DV = 128
GLA_RANK = 16
GLA_GATE_NORM = 16.0
HG_HEADS = 4
HG_EXPAND = 128
HG_DV = 128
REC_BLOCK = CHUNK // 2
XA_HEADS = 4
XA_HEAD_DIM = 128
XA_WIDTH = XA_HEADS * XA_HEAD_DIM
N_BRANCH = 3
D_FF = -(-8 * D_MODEL // (3 * 256)) * 256
LN_EPS = 1e-5
DEEPNORM_ALPHA = (2 * DEPTH) ** 0.25
DEEPNORM_BETA = (8 * DEPTH) ** -0.25
IN_WIDTHS = (SB_WIDTH, SB_WIDTH, SB_WIDTH,
             GLA_HEADS * GLA_DK, GLA_HEADS * GLA_DK, GLA_HEADS * GLA_DV, GLA_HEADS * GLA_DV, GLA_RANK,
             HG_HEADS * HG_EXPAND, HG_HEADS * HG_EXPAND, HG_HEADS * HG_DV, HG_HEADS * HG_DV,
             N_BRANCH * D_MODEL)
IN_TOTAL = sum(IN_WIDTHS)
IN_OFFSETS = tuple(int(o) for o in np.cumsum(IN_WIDTHS)[:-1])

kernel_name = 'hybrid_stream_encoder_step'


def layer_norm(x, g, b):
    xf = x.astype(jnp.float32)
    mu = xf.mean(-1, keepdims=True)
    var = jnp.square(xf - mu).mean(-1, keepdims=True)
    return ((xf - mu) * lax.rsqrt(var + LN_EPS) * g + b).astype(x.dtype)


def head_rms_norm(o, g):
    return o * lax.rsqrt(jnp.mean(jnp.square(o), -1, keepdims=True) + LN_EPS) * g.astype(jnp.float32)


def sb_block(q, k, v, q_pos, k_pos):
    z = jnp.einsum('bqhd,bkhd->bhqk', q, k, preferred_element_type=jnp.float32) * (SB_HEAD_DIM ** -0.5)
    visible = k_pos[None, :] < q_pos[:, None]
    log_keep = jnp.where(visible, jax.nn.log_sigmoid(-z), 0.0)
    later = lax.cumsum(log_keep, axis=3, reverse=True) - log_keep
    w = jnp.where(visible, jnp.exp(jax.nn.log_sigmoid(z) + later), 0.0)
    return jnp.einsum('bhqk,bkhd->bqhd', w.astype(v.dtype), v)


def stick_breaking_attention(q, k, v, q_pos, k_pos):
    b, tq, h, d = q.shape
    if tq <= SB_BLOCK:
        return sb_block(q, k, v, q_pos, k_pos)
    n = tq // SB_BLOCK
    qb = q.reshape(b, n, SB_BLOCK, h, d).transpose(1, 0, 2, 3, 4)
    pb = q_pos.reshape(n, SB_BLOCK)
    out = lax.map(lambda a: sb_block(a[0], k, v, a[1], k_pos), (qb, pb))
    return out.transpose(1, 0, 2, 3, 4).reshape(b, tq, h, d)


def gated_linear_attention(q, k, v, log_g, s0):
    f32 = jnp.float32
    b, t, h, kd = q.shape
    vd = v.shape[-1]
    blk = min(REC_BLOCK, t)
    pad = (-t) % blk
    padw = ((0, 0), (0, pad), (0, 0), (0, 0))
    q, k, v, log_g = (jnp.pad(a.astype(f32), padw) for a in (q, k, v, log_g))
    n = (t + pad) // blk
    q, k, log_g = (a.reshape(b, n, blk, h, kd) for a in (q, k, log_g))
    v = v.reshape(b, n, blk, h, vd)
    cum = jnp.cumsum(log_g, axis=2)
    cum_last = cum[:, :, -1:]
    q_dec = q * jnp.exp(cum)
    k_dec = k * jnp.exp(-cum)
    k_end = k * jnp.exp(cum_last - cum)
    causal = jnp.tril(jnp.ones((blk, blk), bool))
    scores = jnp.where(causal, jnp.einsum('bnthk,bnshk->bnhts', q_dec, k_dec), 0.0)
    o_intra = jnp.einsum('bnhts,bnshv->bnthv', scores, v)
    ds = jnp.einsum('bnshk,bnshv->nbhkv', k_end, v)
    decay = jnp.exp(cum_last[:, :, 0]).transpose(1, 0, 2, 3)

    def step(s, inp):
        d, dsi = inp
        return d[..., None] * s + dsi, s

    s_final, s_start = lax.scan(step, s0.astype(f32), (decay, ds))
    o_inter = jnp.einsum('bnthk,nbhkv->bnthv', q_dec, s_start)
    o = (o_intra + o_inter).reshape(b, n * blk, h, vd)[:, :t]
    return o, s_final


def cross_attend(x, mem_k, mem_v, w_xq, w_xo):
    b, t, _ = x.shape
    q = (x @ w_xq).reshape(b, t, XA_HEADS, XA_HEAD_DIM)
    s = jnp.einsum('bthd,bmhd->bhtm', q, mem_k, preferred_element_type=jnp.float32) * (XA_HEAD_DIM ** -0.5)
    p = jax.nn.softmax(s, axis=-1)
    o = jnp.einsum('bhtm,bmhd->bthd', p.astype(mem_v.dtype), mem_v).reshape(b, t, XA_WIDTH)
    return o @ w_xo


def trunk_layer(x, lw, mem_k, mem_v, s_gla, s_hg, past_k, past_v):
    b, t, _ = x.shape
    f32 = jnp.float32
    (sq, sk, sv, gq, gk, gv, gg, ga, hf, hq, hi, hgo, gates) = jnp.split(x @ lw['w_in'], IN_OFFSETS, axis=-1)
    heads = lambda a, nh: a.reshape(b, t, nh, -1)
    sq, sk, sv = heads(sq, SB_HEADS), heads(sk, SB_HEADS), heads(sv, SB_HEADS)
    if past_k is None:
        keys, vals, n_past = sk, sv, 0
    else:
        keys = jnp.concatenate([past_k.astype(sk.dtype), sk], axis=1)
        vals = jnp.concatenate([past_v.astype(sv.dtype), sv], axis=1)
        n_past = past_k.shape[1]
    q_pos = n_past + jnp.arange(t, dtype=jnp.int32)
    k_pos = jnp.arange(n_past + t, dtype=jnp.int32)
    o_a = stick_breaking_attention(sq, keys, vals, q_pos, k_pos).reshape(b, t, SB_WIDTH)
    log_a = jax.nn.log_sigmoid((ga @ lw['gla_w_a'] + lw['gla_b_a']).astype(f32)) / GLA_GATE_NORM
    o_b, s_gla_new = gated_linear_attention(heads(gq, GLA_HEADS) * (GLA_DK ** -0.5), heads(gk, GLA_HEADS),
                                            heads(gv, GLA_HEADS), heads(log_a, GLA_HEADS), s_gla)
    o_b = (head_rms_norm(o_b, lw['gla_norm_g']) * jax.nn.silu(heads(gg, GLA_HEADS).astype(f32))).reshape(b, t, -1)
    lb = lw['hg_lb'].reshape(HG_HEADS, HG_EXPAND)
    zf = heads(hf, HG_HEADS).astype(f32)
    log_f = jnp.logaddexp(jnp.log(lb), jnp.log1p(-lb) + jax.nn.log_sigmoid(zf))
    k_c = (1.0 - lb) * jax.nn.sigmoid(-zf)
    q_c = jax.nn.silu(heads(hq, HG_HEADS).astype(f32)) * (HG_EXPAND ** -0.5)
    o_c, s_hg_new = gated_linear_attention(q_c, k_c, heads(hi, HG_HEADS), log_f, s_hg)
    o_c = (head_rms_norm(o_c, lw['hg_norm_g']) * jax.nn.silu(heads(hgo, HG_HEADS).astype(f32))).reshape(b, t, -1)
    g_a, g_b, g_c = jnp.split(jax.nn.sigmoid(gates), N_BRANCH, axis=-1)
    m = (g_a * (o_a @ lw['w_up_sb']) + g_b * (o_b.astype(x.dtype) @ lw['w_up_gla'])
         + g_c * (o_c.astype(x.dtype) @ lw['w_up_hg']))
    x = layer_norm(DEEPNORM_ALPHA * x + m @ lw['w_o'], lw['ln1_g'], lw['ln1_b'])
    x = layer_norm(DEEPNORM_ALPHA * x + cross_attend(x, mem_k, mem_v, lw['w_xq'], lw['w_xo']), lw['ln2_g'], lw['ln2_b'])
    gate, up = jnp.split(x @ lw['w_ffn_in'], 2, axis=-1)
    x = layer_norm(DEEPNORM_ALPHA * x + (jax.nn.silu(gate) * up) @ lw['w_ffn_out'], lw['ln3_g'], lw['ln3_b'])
    return x, sk, sv, s_gla_new, s_hg_new


def setup_inputs(seed: int = 0) -> dict:
    key = jax.random.key(seed)
    ks = jax.random.split(key, 40)
    nrm = lambda k, shape, s=1.0: jax.random.normal(k, shape, jnp.float32) * s
    gain = lambda k, shape: 1.0 + nrm(k, shape, 0.01)
    beta = DEEPNORM_BETA
    return {
        'x_prompt': nrm(ks[0], (BATCH, SEQ, D_MODEL)),
        'x_sample': nrm(ks[1], (DEC_BATCH, DEC_SEQ, D_MODEL)),
        'mem_prompt': nrm(ks[2], (BATCH, N_MEM, D_MODEL)),
        'cache_sb_k': nrm(ks[3], (DEPTH, DEC_BATCH, PAST_LEN, SB_HEADS, SB_HEAD_DIM)),
        'cache_sb_v': nrm(ks[4], (DEPTH, DEC_BATCH, PAST_LEN, SB_HEADS, SB_HEAD_DIM)),
        'cache_mem_k': nrm(ks[5], (DEPTH, DEC_BATCH, N_MEM, XA_HEADS, XA_HEAD_DIM)),
        'cache_mem_v': nrm(ks[6], (DEPTH, DEC_BATCH, N_MEM, XA_HEADS, XA_HEAD_DIM)),
        'state_gla': nrm(ks[7], (DEPTH, DEC_BATCH, GLA_HEADS, GLA_DK, GLA_DV)),
        'state_hgrn': nrm(ks[8], (DEPTH, DEC_BATCH, HG_HEADS, HG_EXPAND, HG_DV)),
        'ln_in_g': gain(ks[9], (D_MODEL,)),
        'ln_in_b': nrm(ks[10], (D_MODEL,), 0.01),
        'w_in': nrm(ks[11], (DEPTH, D_MODEL, IN_TOTAL), D_MODEL ** -0.5),
        'gla_w_a': nrm(ks[12], (DEPTH, GLA_RANK, GLA_HEADS * GLA_DK), GLA_RANK ** -0.5),
        'gla_b_a': nrm(ks[13], (DEPTH, GLA_HEADS * GLA_DK), 0.01),
        'gla_norm_g': gain(ks[14], (DEPTH, GLA_DV)),
        'hg_lower': nrm(ks[15], (DEPTH, HG_HEADS * HG_EXPAND), 0.1),
        'hg_norm_g': gain(ks[16], (DEPTH, HG_DV)),
        'w_up_sb': nrm(ks[17], (DEPTH, SB_WIDTH, D_MODEL), SB_WIDTH ** -0.5),
        'w_up_gla': nrm(ks[18], (DEPTH, GLA_HEADS * GLA_DV, D_MODEL), (GLA_HEADS * GLA_DV) ** -0.5),
        'w_up_hg': nrm(ks[19], (DEPTH, HG_HEADS * HG_DV, D_MODEL), (HG_HEADS * HG_DV) ** -0.5),
        'w_o': nrm(ks[20], (DEPTH, D_MODEL, D_MODEL), beta * D_MODEL ** -0.5),
        'ln1_g': gain(ks[21], (DEPTH, D_MODEL)),
        'ln1_b': nrm(ks[22], (DEPTH, D_MODEL), 0.01),
        'w_xq': nrm(ks[23], (DEPTH, D_MODEL, XA_WIDTH), D_MODEL ** -0.5),
        'w_xk': nrm(ks[24], (DEPTH, D_MODEL, XA_WIDTH), D_MODEL ** -0.5),
        'w_xv': nrm(ks[25], (DEPTH, D_MODEL, XA_WIDTH), beta * D_MODEL ** -0.5),
        'w_xo': nrm(ks[26], (DEPTH, XA_WIDTH, D_MODEL), beta * XA_WIDTH ** -0.5),
        'ln2_g': gain(ks[27], (DEPTH, D_MODEL)),
        'ln2_b': nrm(ks[28], (DEPTH, D_MODEL), 0.01),
        'w_ffn_in': nrm(ks[29], (DEPTH, D_MODEL, 2 * D_FF), D_MODEL ** -0.5),
        'w_ffn_out': nrm(ks[30], (DEPTH, D_FF, D_MODEL), beta * D_FF ** -0.5),
        'ln3_g': gain(ks[31], (DEPTH, D_MODEL)),
        'ln3_b': nrm(ks[32], (DEPTH, D_MODEL), 0.01),
    }


def reference(x_prompt, x_sample, mem_prompt, cache_sb_k, cache_sb_v, cache_mem_k, cache_mem_v,
              state_gla, state_hgrn, ln_in_g, ln_in_b, w_in, gla_w_a, gla_b_a, gla_norm_g, hg_lower,
              hg_norm_g, w_up_sb, w_up_gla, w_up_hg, w_o, ln1_g, ln1_b, w_xq, w_xk, w_xv, w_xo,
              ln2_g, ln2_b, w_ffn_in, w_ffn_out, ln3_g, ln3_b):
    f32 = jnp.float32
    lb_all = jnp.cumsum(jax.nn.softmax(hg_lower.astype(f32), axis=0), axis=0)
    lb_all = lb_all - lb_all[0:1]
    xp = layer_norm(x_prompt, ln_in_g, ln_in_b)
    xs = layer_norm(x_sample, ln_in_g, ln_in_b)
    bp = x_prompt.shape[0]
    zero_gla = jnp.zeros((bp, GLA_HEADS, GLA_DK, GLA_DV), f32)
    zero_hg = jnp.zeros((bp, HG_HEADS, HG_EXPAND, HG_DV), f32)
    outs_p, outs_s = [], []
    for l in range(DEPTH):
        lw = {'w_in': w_in[l], 'gla_w_a': gla_w_a[l], 'gla_b_a': gla_b_a[l], 'gla_norm_g': gla_norm_g[l],
              'hg_lb': lb_all[l], 'hg_norm_g': hg_norm_g[l], 'w_up_sb': w_up_sb[l], 'w_up_gla': w_up_gla[l],
              'w_up_hg': w_up_hg[l], 'w_o': w_o[l], 'ln1_g': ln1_g[l], 'ln1_b': ln1_b[l], 'w_xq': w_xq[l],
              'w_xo': w_xo[l], 'ln2_g': ln2_g[l], 'ln2_b': ln2_b[l], 'w_ffn_in': w_ffn_in[l],
              'w_ffn_out': w_ffn_out[l], 'ln3_g': ln3_g[l], 'ln3_b': ln3_b[l]}
        mem_k = (mem_prompt @ w_xk[l]).reshape(bp, -1, XA_HEADS, XA_HEAD_DIM)
        mem_v = (mem_prompt @ w_xv[l]).reshape(bp, -1, XA_HEADS, XA_HEAD_DIM)
        xp, kp, vp, gp, hp = trunk_layer(xp, lw, mem_k, mem_v, zero_gla, zero_hg, None, None)
        xs, ksa, vsa, gsa, hsa = trunk_layer(xs, lw, cache_mem_k[l], cache_mem_v[l], state_gla[l],
                                             state_hgrn[l], cache_sb_k[l], cache_sb_v[l])
        outs_p.append((kp, vp, mem_k, mem_v, gp, hp))
        outs_s.append((ksa, vsa, gsa, hsa))
    stk = lambda outs, i: jnp.stack([o[i] for o in outs], axis=0)
    return (xp, xs, stk(outs_p, 0), stk(outs_p, 1), stk(outs_p, 2), stk(outs_p, 3), stk(outs_p, 4),
            stk(outs_p, 5), stk(outs_s, 0), stk(outs_s, 1), stk(outs_s, 2), stk(outs_s, 3))
```

```python
import functools

import jax
import jax.numpy as jnp
from jax import lax
from jax.experimental import pallas as pl
from jax.experimental.pallas import tpu as pltpu

F32 = jnp.float32
BF16 = jnp.bfloat16

D_MODEL = 2048
DEPTH = 2
HEAD_DIM = 128
SB_HEADS = 8
SB_WIDTH = SB_HEADS * HEAD_DIM
REC_HEADS = 4
REC_WIDTH = REC_HEADS * HEAD_DIM
GLA_RANK = 16
GLA_GATE_NORM = 16.0
XA_HEADS = 4
XA_WIDTH = XA_HEADS * HEAD_DIM
D_FF = 5632
LN_EPS = 1e-5
ALPHA = (2 * DEPTH) ** 0.25
ATT_SCALE = HEAD_DIM ** -0.5

C_GATE = 0
C_SQ = 3 * D_MODEL
C_SK = C_SQ + SB_WIDTH
C_SV = C_SK + SB_WIDTH
C_GQ = C_SV + SB_WIDTH
C_GK = C_GQ + REC_WIDTH
C_GV = C_GK + REC_WIDTH
C_GG = C_GV + REC_WIDTH
C_GA = C_GG + REC_WIDTH
GA_PAD = 512
C_HF = C_GA + GA_PAD
C_HQ = C_HF + REC_WIDTH
C_HI = C_HQ + REC_WIDTH
C_HO = C_HI + REC_WIDTH
N_PROJ = C_HO + REC_WIDTH

SB_BLK = 128
REC_BLK = 64
VMEM_LIMIT = 56 * 1024 * 1024


def _cparams(n_axes):
    return pltpu.CompilerParams(dimension_semantics=("arbitrary",) * n_axes,
                                vmem_limit_bytes=VMEM_LIMIT)


def _ln_rows(x, g, b):
    mu = jnp.mean(x, axis=-1, keepdims=True)
    xc = x - mu
    var = jnp.mean(xc * xc, axis=-1, keepdims=True)
    return xc * lax.rsqrt(var + LN_EPS) * g + b


def _log_sigmoid(z):
    return jnp.minimum(z, 0.0) - jnp.log1p(jnp.exp(-jnp.abs(z)))


def _sigmoid(z):
    return 1.0 / (1.0 + jnp.exp(-z))


def _dot(a, b):
    return jnp.dot(a, b, preferred_element_type=F32)


def _dot_nt(a, b):
    return lax.dot_general(a, b, (((1,), (1,)), ((), ())), preferred_element_type=F32)


def _dot_tn(a, b):
    return lax.dot_general(a, b, (((0,), (0,)), ((), ())), preferred_element_type=F32)


def _split_bf16(x):
    hi = x.astype(BF16)
    lo = (x - hi.astype(F32)).astype(BF16)
    return hi, lo


def _proj_kernel(*refs, apply_ln, row_chunk):
    if apply_ln:
        x_ref, g_ref, b_ref, w_ref, o_ref, xn_ref, xs_ref = refs
    else:
        x_ref, w_ref, o_ref, xs_ref = refs

    @pl.when(pl.program_id(1) == 0)
    def _():
        def body(r, carry):
            rows = pl.ds(pl.multiple_of(r * row_chunk, row_chunk), row_chunk)
            x = x_ref[rows, :]
            if apply_ln:
                x = _ln_rows(x, g_ref[...], b_ref[...])
                xn_ref[rows, :] = x
            xs_ref[rows, :] = x.astype(BF16)
            return carry
        lax.fori_loop(0, x_ref.shape[0] // row_chunk, body, 0)

    o_ref[...] = _dot(xs_ref[...], w_ref[...])


def _proj(x, w, ln=None, *, tm, tn):
    t, d = x.shape
    n = w.shape[1]
    apply_ln = ln is not None
    x_spec = pl.BlockSpec((tm, d), lambda i, j: (i, 0))
    vec_spec = pl.BlockSpec((1, d), lambda i, j: (0, 0))
    w_spec = pl.BlockSpec((d, tn), lambda i, j: (0, j))
    o_spec = pl.BlockSpec((tm, tn), lambda i, j: (i, j))
    kern = functools.partial(_proj_kernel, apply_ln=apply_ln, row_chunk=min(128, tm))
    if apply_ln:
        g, b = ln
        return pl.pallas_call(
            kern, grid=(t // tm, n // tn),
            in_specs=[x_spec, vec_spec, vec_spec, w_spec],
            out_specs=[o_spec, x_spec],
            out_shape=[jax.ShapeDtypeStruct((t, n), F32), jax.ShapeDtypeStruct((t, d), F32)],
            scratch_shapes=[pltpu.VMEM((tm, d), BF16)],
            compiler_params=_cparams(2), name="proj_ln",
        )(x, g.reshape(1, d), b.reshape(1, d), w)
    return pl.pallas_call(
        kern, grid=(t // tm, n // tn),
        in_specs=[x_spec, w_spec], out_specs=o_spec,
        out_shape=jax.ShapeDtypeStruct((t, n), F32),
        scratch_shapes=[pltpu.VMEM((tm, d), BF16)],
        compiler_params=_cparams(2), name="proj",
    )(x, w)


def _sb_block(qb, kb, vb, mt, carry, acc, mask):
    nk = kb.shape[0]
    z = _dot_nt(qb, kb) * ATT_SCALE
    ls = _log_sigmoid(z)
    lk = ls - z
    if mask is not None:
        lk = jnp.where(mask, lk, 0.0)
    hi, lo = _split_bf16(lk)
    m = mt[:nk, :]
    lt = _dot(hi, m) + _dot(lo, m)
    w = jnp.exp(ls + lt[:, :nk] + carry[:, :nk])
    if mask is not None:
        w = jnp.where(mask, w, 0.0)
    acc = acc + _dot(w.astype(BF16), vb)
    carry = carry + lt[:, SB_BLK:]
    return carry, acc


def _strict_lower(n):
    row = lax.broadcasted_iota(jnp.int32, (n, n), 0)
    col = lax.broadcasted_iota(jnp.int32, (n, n), 1)
    return col < row


def _sb_prompt_kernel(q_ref, k_ref, v_ref, mt_ref, o_ref, kb_ref, vb_ref):
    i = pl.program_id(2)

    @pl.when(i == 0)
    def _():
        kb_ref[...] = k_ref[...].astype(BF16)
        vb_ref[...] = v_ref[...].astype(BF16)

    qb = q_ref[...].astype(BF16)
    mt = mt_ref[...]
    zero = jnp.zeros((SB_BLK, SB_BLK), F32)
    diag = pl.ds(pl.multiple_of(i * SB_BLK, SB_BLK), SB_BLK)
    carry, acc = _sb_block(qb, kb_ref[diag, :], vb_ref[diag, :], mt, zero, zero,
                           _strict_lower(SB_BLK))

    def body(s, c):
        rows = pl.ds(pl.multiple_of((i - 1 - s) * SB_BLK, SB_BLK), SB_BLK)
        return _sb_block(qb, kb_ref[rows, :], vb_ref[rows, :], mt, c[0], c[1], None)

    carry, acc = lax.fori_loop(0, i, body, (carry, acc))
    o_ref[...] = acc.astype(o_ref.dtype)


def _sb_prompt(proj, mt, batch, seq):
    nq = seq // SB_BLK
    cq, ck, cv = C_SQ // HEAD_DIM, C_SK // HEAD_DIM, C_SV // HEAD_DIM
    return pl.pallas_call(
        _sb_prompt_kernel, grid=(batch, SB_HEADS, nq),
        in_specs=[
            pl.BlockSpec((SB_BLK, HEAD_DIM), lambda b, h, i: (b * nq + i, cq + h)),
            pl.BlockSpec((seq, HEAD_DIM), lambda b, h, i: (b, ck + h)),
            pl.BlockSpec((seq, HEAD_DIM), lambda b, h, i: (b, cv + h)),
            pl.BlockSpec((SB_BLK, 2 * SB_BLK), lambda b, h, i: (0, 0)),
        ],
        out_specs=pl.BlockSpec((SB_BLK, HEAD_DIM), lambda b, h, i: (b * nq + i, h)),
        out_shape=jax.ShapeDtypeStruct((batch * seq, SB_WIDTH), BF16),
        scratch_shapes=[pltpu.VMEM((seq, HEAD_DIM), BF16), pltpu.VMEM((seq, HEAD_DIM), BF16)],
        compiler_params=_cparams(3), name="sb_prompt",
    )(proj, proj, proj, mt)


def _sb_sample_kernel(q_ref, kn_ref, vn_ref, kp_ref, vp_ref, mt_ref, o_ref, carry_ref, acc_ref):
    c = pl.program_id(1)
    tq = q_ref.shape[0]
    n_blk = kp_ref.shape[0] // (SB_BLK * SB_HEADS)
    mt = mt_ref[...]

    @pl.when(c == 0)
    def _():
        zero = jnp.zeros((tq, SB_BLK), F32)
        mask = _strict_lower(tq)
        for h in range(SB_HEADS):
            cs = slice(h * HEAD_DIM, (h + 1) * HEAD_DIM)
            carry, acc = _sb_block(q_ref[:, cs].astype(BF16), kn_ref[:, cs].astype(BF16),
                                   vn_ref[:, cs].astype(BF16), mt, zero, zero, mask)
            carry_ref[h] = carry
            acc_ref[h] = acc

    for h in range(SB_HEADS):
        cs = slice(h * HEAD_DIM, (h + 1) * HEAD_DIM)
        qb = q_ref[:, cs].astype(BF16)

        def body(s, cr, h=h, qb=qb):
            start = pl.multiple_of((n_blk - 1 - s) * (SB_BLK * SB_HEADS), SB_BLK * SB_HEADS) + h
            rows = pl.ds(start, SB_BLK, stride=SB_HEADS)
            return _sb_block(qb, kp_ref[rows, :].astype(BF16), vp_ref[rows, :].astype(BF16),
                             mt, cr[0], cr[1], None)

        carry, acc = lax.fori_loop(0, n_blk, body, (carry_ref[h], acc_ref[h]))
        carry_ref[h] = carry
        acc_ref[h] = acc

    @pl.when(c == pl.num_programs(1) - 1)
    def _():
        for h in range(SB_HEADS):
            o_ref[:, h * HEAD_DIM:(h + 1) * HEAD_DIM] = acc_ref[h].astype(o_ref.dtype)


def _sb_sample(proj, past_k, past_v, mt, batch, tq, *, chunk):
    past = past_k.shape[1] // SB_HEADS
    n_chunks = past // chunk
    col = lambda c0: pl.BlockSpec((tq, SB_WIDTH), lambda b, c, c0=c0: (b, c0 // SB_WIDTH))
    cache = pl.BlockSpec((None, chunk * SB_HEADS, HEAD_DIM), lambda b, c: (b, n_chunks - 1 - c, 0))
    return pl.pallas_call(
        _sb_sample_kernel, grid=(batch, n_chunks),
        in_specs=[col(C_SQ), col(C_SK), col(C_SV), cache, cache,
                  pl.BlockSpec((SB_BLK, 2 * SB_BLK), lambda b, c: (0, 0))],
        out_specs=pl.BlockSpec((tq, SB_WIDTH), lambda b, c: (b, 0)),
        out_shape=jax.ShapeDtypeStruct((batch * tq, SB_WIDTH), BF16),
        scratch_shapes=[pltpu.VMEM((SB_HEADS, tq, SB_BLK), F32),
                        pltpu.VMEM((SB_HEADS, tq, HEAD_DIM), F32)],
        compiler_params=_cparams(2), name="sb_sample",
    )(proj, proj, proj, past_k, past_v, mt)


def _rec_heads(q, k, v, log_g, gate, norm_g, lbd, st_ref, o_ref):
    tc = q.shape[0]
    hi, lo = _split_bf16(log_g)
    cum = _dot(lbd, hi) + _dot(lbd, lo)
    row = lax.broadcasted_iota(jnp.int32, (REC_BLK, REC_BLK), 0)
    col = lax.broadcasted_iota(jnp.int32, (REC_BLK, REC_BLK), 1)
    causal = col <= row
    half = REC_BLK // 2
    for h in range(REC_HEADS):
        cs = slice(h * HEAD_DIM, (h + 1) * HEAD_DIM)
        for n in range(tc // REC_BLK):
            rs = slice(n * REC_BLK, (n + 1) * REC_BLK)
            c = cum[rs, cs]
            mid = c[half - 1:half, :]
            last = c[REC_BLK - 1:REC_BLK, :]
            q1 = q[rs, cs] * jnp.exp(c - mid)
            k1 = k[rs, cs] * jnp.exp(mid - c)
            vb = v[rs, cs].astype(BF16)
            s = jnp.where(causal, _dot_nt(q1.astype(BF16), k1.astype(BF16)), 0.0)
            o = _dot(s.astype(BF16), vb)
            st = st_ref[h]
            qd = q1 * jnp.exp(mid)
            o = o + _dot_nt(qd.astype(BF16), st.astype(BF16))
            ke = k1 * jnp.exp(last - mid)
            st_ref[h] = st * jnp.exp(last) + _dot_tn(vb, ke.astype(BF16))
            ms = jnp.mean(o * o, axis=-1, keepdims=True)
            o = o * lax.rsqrt(ms + LN_EPS) * norm_g
            g = gate[rs, cs]
            o_ref[rs, cs] = (o * (g * _sigmoid(g))).astype(o_ref.dtype)


def _rec_state_io(s0_ref, sout_ref, st_ref, body):
    c = pl.program_id(1)

    @pl.when(c == 0)
    def _():
        for h in range(REC_HEADS):
            st_ref[h] = s0_ref[0, h].T

    body()

    @pl.when(c == pl.num_programs(1) - 1)
    def _():
        for h in range(REC_HEADS):
            sout_ref[0, h] = st_ref[h].T


def _gla_kernel(q_ref, k_ref, v_ref, g_ref, a_ref, wa_ref, ba_ref, ng_ref, lbd_ref, s0_ref,
                o_ref, sout_ref, st_ref):
    def body():
        pre = _dot(a_ref[...].astype(BF16), wa_ref[...]) + ba_ref[...]
        log_g = _log_sigmoid(pre) * (1.0 / GLA_GATE_NORM)
        _rec_heads(q_ref[...] * ATT_SCALE, k_ref[...], v_ref[...], log_g, g_ref[...],
                   ng_ref[...], lbd_ref[...], st_ref, o_ref)
    _rec_state_io(s0_ref, sout_ref, st_ref, body)


def _hgrn_kernel(f_ref, q_ref, i_ref, g_ref, lb_ref, ng_ref, lbd_ref, s0_ref,
                 o_ref, sout_ref, st_ref):
    def body():
        lb = lb_ref[...]
        zf = f_ref[...]
        a = jnp.log(lb)
        b = jnp.log1p(-lb) + _log_sigmoid(zf)
        log_f = jnp.maximum(a, b) + jnp.log1p(jnp.exp(-jnp.abs(a - b)))
        k = (1.0 - lb) * _sigmoid(-zf)
        hq = q_ref[...]
        q = hq * _sigmoid(hq) * ATT_SCALE
        _rec_heads(q, k, i_ref[...], log_f, g_ref[...], ng_ref[...], lbd_ref[...], st_ref, o_ref)
    _rec_state_io(s0_ref, sout_ref, st_ref, body)


def _rec_specs(batch, t_b, tc):
    nc = t_b // tc
    col = lambda c0: pl.BlockSpec((tc, REC_WIDTH), lambda b, c, c0=c0: (b * nc + c, c0 // REC_WIDTH))
    full = lambda shape: pl.BlockSpec(shape, lambda b, c: (0,) * len(shape))
    state = pl.BlockSpec((1, REC_HEADS, HEAD_DIM, HEAD_DIM), lambda b, c: (b, 0, 0, 0))
    out_specs = [pl.BlockSpec((tc, REC_WIDTH), lambda b, c: (b * nc + c, 0)), state]
    out_shape = [jax.ShapeDtypeStruct((batch * t_b, REC_WIDTH), BF16),
                 jax.ShapeDtypeStruct((batch, REC_HEADS, HEAD_DIM, HEAD_DIM), F32)]
    scratch = [pltpu.VMEM((REC_HEADS, HEAD_DIM, HEAD_DIM), F32)]
    return nc, col, full, state, out_specs, out_shape, scratch


def _gla(proj, wa, ba, ng, lbd, s0, batch, t_b, tc):
    nc, col, full, state, out_specs, out_shape, scratch = _rec_specs(batch, t_b, tc)
    a_spec = pl.BlockSpec((tc, HEAD_DIM), lambda b, c: (b * nc + c, C_GA // HEAD_DIM))
    return pl.pallas_call(
        _gla_kernel, grid=(batch, nc),
        in_specs=[col(C_GQ), col(C_GK), col(C_GV), col(C_GG), a_spec,
                  full((HEAD_DIM, REC_WIDTH)), full((1, REC_WIDTH)), full((1, HEAD_DIM)),
                  full((tc, tc)), state],
        out_specs=out_specs, out_shape=out_shape, scratch_shapes=scratch,
        compiler_params=_cparams(2), name="gla",
    )(proj, proj, proj, proj, proj, wa, ba, ng, lbd, s0)


def _hgrn(proj, lb, ng, lbd, s0, batch, t_b, tc):
    nc, col, full, state, out_specs, out_shape, scratch = _rec_specs(batch, t_b, tc)
    return pl.pallas_call(
        _hgrn_kernel, grid=(batch, nc),
        in_specs=[col(C_HF), col(C_HQ), col(C_HI), col(C_HO),
                  full((1, REC_WIDTH)), full((1, HEAD_DIM)), full((tc, tc)), state],
        out_specs=out_specs, out_shape=out_shape, scratch_shapes=scratch,
        compiler_params=_cparams(2), name="hgrn",
    )(proj, proj, proj, proj, lb, ng, lbd, s0)


def _mix_kernel(x_ref, ga_ref, gb_ref, gc_ref, oa_ref, ob_ref, oc_ref,
                wsb_ref, wgla_ref, whg_ref, wo_ref, g_ref, b_ref, o_ref):
    m = _sigmoid(ga_ref[...]) * _dot(oa_ref[...], wsb_ref[...])
    m = m + _sigmoid(gb_ref[...]) * _dot(ob_ref[...], wgla_ref[...])
    m = m + _sigmoid(gc_ref[...]) * _dot(oc_ref[...], whg_ref[...])
    y = ALPHA * x_ref[...] + _dot(m.astype(BF16), wo_ref[...])
    o_ref[...] = _ln_rows(y, g_ref[...], b_ref[...])


def _resident(shape):
    return pl.BlockSpec(shape, lambda *_: (0,) * len(shape), pipeline_mode=pl.Buffered(1))


def _mix(x, proj, oa, ob, oc, wsb, wgla, whg, wo, g, b, *, tm):
    t, d = x.shape
    row = lambda w, j=0: pl.BlockSpec((tm, w), lambda i, j=j: (i, j))
    return pl.pallas_call(
        _mix_kernel, grid=(t // tm,),
        in_specs=[row(d), row(d, 0), row(d, 1), row(d, 2),
                  row(SB_WIDTH), row(REC_WIDTH), row(REC_WIDTH),
                  _resident((SB_WIDTH, d)), _resident((REC_WIDTH, d)), _resident((REC_WIDTH, d)),
                  _resident((d, d)), _resident((1, d)), _resident((1, d))],
        out_specs=row(d),
        out_shape=jax.ShapeDtypeStruct((t, d), F32),
        compiler_params=_cparams(1), name="mix",
    )(x, proj, proj, proj, oa, ob, oc, wsb, wgla, whg, wo, g.reshape(1, d), b.reshape(1, d))


def _xattn_kernel(x_ref, mk_ref, mv_ref, wq_ref, wo_ref, g_ref, b_ref, o_ref):
    x = x_ref[...]
    q = _dot(x.astype(BF16), wq_ref[...])
    outs = []
    for h in range(XA_HEADS):
        cs = slice(h * HEAD_DIM, (h + 1) * HEAD_DIM)
        s = _dot_nt(q[:, cs].astype(BF16), mk_ref[:, cs].astype(BF16)) * ATT_SCALE
        s = s - jnp.max(s, axis=-1, keepdims=True)
        e = jnp.exp(s)
        p = e / jnp.sum(e, axis=-1, keepdims=True)
        outs.append(_dot(p.astype(BF16), mv_ref[:, cs].astype(BF16)))
    o = jnp.concatenate(outs, axis=-1)
    y = ALPHA * x + _dot(o.astype(BF16), wo_ref[...])
    o_ref[...] = _ln_rows(y, g_ref[...], b_ref[...])


def _xattn(x, mem_k, mem_v, wq, wo, g, b, *, batch, tm):
    t, d = x.shape
    nt = (t // batch) // tm
    n_mem = mem_k.shape[1]
    row = pl.BlockSpec((tm, d), lambda bi, i: (bi * nt + i, 0))
    mem = pl.BlockSpec((None, n_mem, XA_WIDTH), lambda bi, i: (bi, 0, 0))
    return pl.pallas_call(
        _xattn_kernel, grid=(batch, nt),
        in_specs=[row, mem, mem, _resident((d, XA_WIDTH)), _resident((XA_WIDTH, d)),
                  _resident((1, d)), _resident((1, d))],
        out_specs=row,
        out_shape=jax.ShapeDtypeStruct((t, d), F32),
        compiler_params=_cparams(2), name="xattn",
    )(x, mem_k, mem_v, wq, wo, g.reshape(1, d), b.reshape(1, d))


def _ffn_kernel(x_ref, wg_ref, wu_ref, wd_ref, g_ref, b_ref, o_ref, xs_ref, acc_ref):
    f = pl.program_id(1)

    @pl.when(f == 0)
    def _():
        xs_ref[...] = x_ref[...].astype(BF16)
        acc_ref[...] = jnp.zeros_like(acc_ref)

    xs = xs_ref[...]
    gate = _dot(xs, wg_ref[...])
    up = _dot(xs, wu_ref[...])
    hidden = gate * _sigmoid(gate) * up
    acc_ref[...] += _dot(hidden.astype(BF16), wd_ref[...])

    @pl.when(f == pl.num_programs(1) - 1)
    def _():
        y = ALPHA * x_ref[...] + acc_ref[...]
        o_ref[...] = _ln_rows(y, g_ref[...], b_ref[...])


def _ffn(x, w_in, w_out, g, b, *, tm, tf):
    t, d = x.shape
    nf = D_FF // tf
    row = pl.BlockSpec((tm, d), lambda i, f: (i, 0))
    vec = pl.BlockSpec((1, d), lambda i, f: (0, 0))
    return pl.pallas_call(
        _ffn_kernel, grid=(t // tm, nf),
        in_specs=[row,
                  pl.BlockSpec((d, tf), lambda i, f: (0, f)),
                  pl.BlockSpec((d, tf), lambda i, f: (0, nf + f)),
                  pl.BlockSpec((tf, d), lambda i, f: (f, 0)),
                  vec, vec],
        out_specs=row,
        out_shape=jax.ShapeDtypeStruct((t, d), F32),
        scratch_shapes=[pltpu.VMEM((tm, d), BF16), pltpu.VMEM((tm, d), F32)],
        compiler_params=_cparams(2), name="ffn",
    )(x, w_in, w_in, w_out, g.reshape(1, d), b.reshape(1, d))


def _pack_w_in(w):
    a_end = 3 * SB_WIDTH + 4 * REC_WIDTH
    ga_end = a_end + GLA_RANK
    h_end = ga_end + 4 * REC_WIDTH
    pad = jnp.zeros((w.shape[0], GA_PAD - GLA_RANK), w.dtype)
    return jnp.concatenate([w[:, h_end:], w[:, :a_end], w[:, a_end:ga_end], pad,
                            w[:, ga_end:h_end]], axis=1).astype(BF16)


def _tri_tables():
    j = lax.broadcasted_iota(jnp.int32, (SB_BLK, SB_BLK), 0)
    s = lax.broadcasted_iota(jnp.int32, (SB_BLK, SB_BLK), 1)
    later = (j > s).astype(BF16)
    return jnp.concatenate([later, jnp.ones((SB_BLK, SB_BLK), BF16)], axis=1)


def _block_cumsum_table(tc):
    r = lax.broadcasted_iota(jnp.int32, (tc, tc), 0)
    c = lax.broadcasted_iota(jnp.int32, (tc, tc), 1)
    return ((c <= r) & (r // REC_BLK == c // REC_BLK)).astype(BF16)


def _layer(x, lw, mem_k, mem_v, s_gla, s_hg, past, ln_in, *, batch, t_b, cfg):
    if ln_in is not None:
        proj, x = _proj(x, lw["w_in"], ln_in, tm=cfg["tm_proj"], tn=cfg["tn_proj"])
    else:
        proj = _proj(x, lw["w_in"], tm=cfg["tm_proj"], tn=cfg["tn_proj"])
    if past is None:
        oa = _sb_prompt(proj, lw["mt"], batch, t_b)
    else:
        oa = _sb_sample(proj, past[0], past[1], lw["mt"], batch, t_b, chunk=cfg["sb_chunk"])
    tc = cfg["tc"]
    lbd = _block_cumsum_table(tc)
    ob, s_gla_new = _gla(proj, lw["wa"], lw["ba"], lw["gla_ng"], lbd, s_gla, batch, t_b, tc)
    oc, s_hg_new = _hgrn(proj, lw["lb"], lw["hg_ng"], lbd, s_hg, batch, t_b, tc)
    x = _mix(x, proj, oa, ob, oc, lw["w_up_sb"], lw["w_up_gla"], lw["w_up_hg"], lw["w_o"],
             lw["ln1_g"], lw["ln1_b"], tm=cfg["tm_mix"])
    x = _xattn(x, mem_k, mem_v, lw["w_xq"], lw["w_xo"], lw["ln2_g"], lw["ln2_b"],
               batch=batch, tm=cfg["tm_xa"])
    x = _ffn(x, lw["w_ffn_in"], lw["w_ffn_out"], lw["ln3_g"], lw["ln3_b"],
             tm=cfg["tm_ffn"], tf=cfg["tf"])
    return x, proj, s_gla_new, s_hg_new


def _tile(n, pref):
    t = min(n, pref)
    while n % t:
        t //= 2
    return t


def kernel(x_prompt, x_sample, mem_prompt, cache_sb_k, cache_sb_v, cache_mem_k, cache_mem_v,
           state_gla, state_hgrn, ln_in_g, ln_in_b, w_in, gla_w_a, gla_b_a, gla_norm_g, hg_lower,
           hg_norm_g, w_up_sb, w_up_gla, w_up_hg, w_o, ln1_g, ln1_b, w_xq, w_xk, w_xv, w_xo,
           ln2_g, ln2_b, w_ffn_in, w_ffn_out, ln3_g, ln3_b):
    bp, seq, d = x_prompt.shape
    bs, tq, _ = x_sample.shape
    n_mem = mem_prompt.shape[1]
    past_len = cache_sb_k.shape[2]
    depth = w_in.shape[0]

    lb_all = jnp.cumsum(jax.nn.softmax(hg_lower.astype(F32), axis=0), axis=0)
    lb_all = lb_all - lb_all[0:1]
    mt = _tri_tables()
    wa_pad = jnp.zeros((depth, HEAD_DIM, REC_WIDTH), BF16).at[:, :GLA_RANK].set(gla_w_a.astype(BF16))

    xp = x_prompt.reshape(bp * seq, d)
    xs = x_sample.reshape(bs * tq, d)
    mem2d = mem_prompt.reshape(bp * n_mem, d)
    zero_state = jnp.zeros((bp, REC_HEADS, HEAD_DIM, HEAD_DIM), F32)

    cfg_p = dict(tm_proj=_tile(bp * seq, 1024), tn_proj=768, tc=_tile(seq, 256),
                 tm_mix=_tile(bp * seq, 256), tm_xa=_tile(seq, 256),
                 tm_ffn=_tile(bp * seq, 512), tf=512)
    cfg_s = dict(tm_proj=_tile(bs * tq, 1024), tn_proj=768, tc=_tile(tq, 256),
                 tm_mix=_tile(bs * tq, 256), tm_xa=_tile(tq, 256),
                 tm_ffn=_tile(bs * tq, 512), tf=512, sb_chunk=_tile(past_len, 1024))

    outs_p, outs_s = [], []
    for l in range(depth):
        lw = dict(
            w_in=_pack_w_in(w_in[l]), mt=mt, wa=wa_pad[l], ba=gla_b_a[l].reshape(1, REC_WIDTH),
            gla_ng=gla_norm_g[l].reshape(1, HEAD_DIM), lb=lb_all[l].reshape(1, REC_WIDTH),
            hg_ng=hg_norm_g[l].reshape(1, HEAD_DIM),
            w_up_sb=w_up_sb[l].astype(BF16), w_up_gla=w_up_gla[l].astype(BF16),
            w_up_hg=w_up_hg[l].astype(BF16), w_o=w_o[l].astype(BF16),
            ln1_g=ln1_g[l], ln1_b=ln1_b[l], w_xq=w_xq[l].astype(BF16), w_xo=w_xo[l].astype(BF16),
            ln2_g=ln2_g[l], ln2_b=ln2_b[l], w_ffn_in=w_ffn_in[l].astype(BF16),
            w_ffn_out=w_ffn_out[l].astype(BF16), ln3_g=ln3_g[l], ln3_b=ln3_b[l])
        ln_in = (ln_in_g, ln_in_b) if l == 0 else None
        tm_mem = _tile(bp * n_mem, 512)
        mem_k = _proj(mem2d, w_xk[l].astype(BF16), tm=tm_mem, tn=XA_WIDTH)
        mem_v = _proj(mem2d, w_xv[l].astype(BF16), tm=tm_mem, tn=XA_WIDTH)
        xp, proj_p, gp, hp = _layer(
            xp, lw, mem_k.reshape(bp, n_mem, XA_WIDTH), mem_v.reshape(bp, n_mem, XA_WIDTH),
            zero_state, zero_state, None, ln_in, batch=bp, t_b=seq, cfg=cfg_p)
        past = (cache_sb_k[l].reshape(bs, past_len * SB_HEADS, HEAD_DIM),
                cache_sb_v[l].reshape(bs, past_len * SB_HEADS, HEAD_DIM))
        xs, proj_s, gs, hs = _layer(
            xs, lw, cache_mem_k[l].reshape(bs, n_mem, XA_WIDTH),
            cache_mem_v[l].reshape(bs, n_mem, XA_WIDTH),
            state_gla[l], state_hgrn[l], past, ln_in, batch=bs, t_b=tq, cfg=cfg_s)
        heads = lambda a, b, t: a.reshape(b, t, SB_HEADS, HEAD_DIM)
        outs_p.append((heads(proj_p[:, C_SK:C_SV], bp, seq), heads(proj_p[:, C_SV:C_GQ], bp, seq),
                       mem_k.reshape(bp, n_mem, XA_HEADS, HEAD_DIM),
                       mem_v.reshape(bp, n_mem, XA_HEADS, HEAD_DIM), gp, hp))
        outs_s.append((heads(proj_s[:, C_SK:C_SV], bs, tq), heads(proj_s[:, C_SV:C_GQ], bs, tq),
                       gs, hs))
    stk = lambda outs, i: jnp.stack([o[i] for o in outs], axis=0)
    return (xp.reshape(bp, seq, d), xs.reshape(bs, tq, d),
            stk(outs_p, 0), stk(outs_p, 1), stk(outs_p, 2), stk(outs_p, 3), stk(outs_p, 4),
            stk(outs_p, 5), stk(outs_s, 0), stk(outs_s, 1), stk(outs_s, 2), stk(outs_s, 3))
```

```python
import functools

import jax
import jax.numpy as jnp
from jax import lax
from jax.experimental import pallas as pl
from jax.experimental.pallas import tpu as pltpu

F32 = jnp.float32
BF16 = jnp.bfloat16

D_MODEL = 2048
DEPTH = 2
HEAD_DIM = 128
SB_HEADS = 8
SB_WIDTH = SB_HEADS * HEAD_DIM
REC_HEADS = 4
REC_WIDTH = REC_HEADS * HEAD_DIM
GLA_RANK = 16
GLA_GATE_NORM = 16.0
XA_HEADS = 4
XA_WIDTH = XA_HEADS * HEAD_DIM
D_FF = 5632
LN_EPS = 1e-5
ALPHA = (2 * DEPTH) ** 0.25
ATT_SCALE = HEAD_DIM ** -0.5

W_SQ = 0
W_SKV = SB_WIDTH
W_REC_A = 3 * SB_WIDTH
W_GA = W_REC_A + 4 * REC_WIDTH
W_REC_H = W_GA + GLA_RANK
W_GATES = W_REC_H + 4 * REC_WIDTH

GA_PAD = 512
R_GQ, R_GK, R_GV, R_GG, R_GA, R_HF, R_HQ, R_HI, R_HO = (i * REC_WIDTH for i in range(9))
N_REC = 9 * REC_WIDTH

SB_BLK = 128
SB_TQ = 256
SB_HEAD_GROUP = 4
REC_BLK = 64
VMEM_LIMIT = 56 * 1024 * 1024


def _cparams(n_axes):
    return pltpu.CompilerParams(dimension_semantics=("arbitrary",) * n_axes,
                                vmem_limit_bytes=VMEM_LIMIT)


def _ln_rows(x, g, b):
    mu = jnp.mean(x, axis=-1, keepdims=True)
    xc = x - mu
    var = jnp.mean(xc * xc, axis=-1, keepdims=True)
    return xc * lax.rsqrt(var + LN_EPS) * g + b


def _log_sigmoid(z):
    return jnp.minimum(z, 0.0) - jnp.log1p(jnp.exp(-jnp.abs(z)))


def _sigmoid(z):
    return 1.0 / (1.0 + jnp.exp(-z))


def _dot(a, b):
    return jnp.dot(a, b, preferred_element_type=F32)


def _dot_nt(a, b):
    return lax.dot_general(a, b, (((1,), (1,)), ((), ())), preferred_element_type=F32)


def _dot_tn(a, b):
    return lax.dot_general(a, b, (((0,), (0,)), ((), ())), preferred_element_type=F32)


def _split_bf16(x):
    hi = x.astype(BF16)
    lo = (x - hi.astype(F32)).astype(BF16)
    return hi, lo


def _resident(shape):
    return pl.BlockSpec(shape, lambda *_: (0,) * len(shape), pipeline_mode=pl.Buffered(1))


def _proj_ln_kernel(x_ref, g_ref, b_ref, w_ref, o_ref, xn_ref, xb_ref, *, row_chunk):
    @pl.when(pl.program_id(1) == 0)
    def _():
        def body(r, carry):
            rows = pl.ds(pl.multiple_of(r * row_chunk, row_chunk), row_chunk)
            x = _ln_rows(x_ref[rows, :], g_ref[...], b_ref[...])
            xn_ref[rows, :] = x
            xb_ref[rows, :] = x.astype(BF16)
            return carry
        lax.fori_loop(0, x_ref.shape[0] // row_chunk, body, 0)

    o_ref[...] = _dot(xb_ref[...], w_ref[...])


def _proj_ln(x, g, b, w, *, tm, tn):
    t, d = x.shape
    n = w.shape[1]
    x_spec = pl.BlockSpec((tm, d), lambda i, j: (i, 0))
    vec_spec = pl.BlockSpec((1, d), lambda i, j: (0, 0))
    return pl.pallas_call(
        functools.partial(_proj_ln_kernel, row_chunk=min(128, tm)), grid=(t // tm, n // tn),
        in_specs=[x_spec, vec_spec, vec_spec, pl.BlockSpec((d, tn), lambda i, j: (0, j))],
        out_specs=[pl.BlockSpec((tm, tn), lambda i, j: (i, j)), x_spec, x_spec],
        out_shape=[jax.ShapeDtypeStruct((t, n), F32), jax.ShapeDtypeStruct((t, d), F32),
                   jax.ShapeDtypeStruct((t, d), BF16)],
        compiler_params=_cparams(2), name="proj_ln",
    )(x, g.reshape(1, d), b.reshape(1, d), w)


def _mm_kernel(x_ref, w_ref, o_ref):
    o_ref[...] = _dot(x_ref[...].astype(BF16), w_ref[...]).astype(o_ref.dtype)


def _mm(x, w, out_dtype, *, tm, tn):
    t, d = x.shape
    n = w.shape[1]
    return pl.pallas_call(
        _mm_kernel, grid=(t // tm, n // tn),
        in_specs=[pl.BlockSpec((tm, d), lambda i, j: (i, 0)),
                  pl.BlockSpec((d, tn), lambda i, j: (0, j))],
        out_specs=pl.BlockSpec((tm, tn), lambda i, j: (i, j)),
        out_shape=jax.ShapeDtypeStruct((t, n), out_dtype),
        compiler_params=_cparams(2), name="mm",
    )(x, w)


def _proj_kv_kernel(x_ref, w_ref, k_all_ref, v_all_ref, k_nat_ref, v_nat_ref, kb_ref, vb_ref):
    del k_all_ref, v_all_ref
    tm = x_ref.shape[0]
    res = _dot(x_ref[...], w_ref[...])
    kb_ref[...] = res[:, :SB_WIDTH].astype(BF16)
    vb_ref[...] = res[:, SB_WIDTH:].astype(BF16)
    for h in range(SB_HEADS):
        rows = pl.ds(h, tm, stride=SB_HEADS)
        k_nat_ref[rows, :] = res[:, h * HEAD_DIM:(h + 1) * HEAD_DIM]
        v_nat_ref[rows, :] = res[:, SB_WIDTH + h * HEAD_DIM:SB_WIDTH + (h + 1) * HEAD_DIM]


def _proj_kv(xb, w, k_all, v_all, layer, *, tm):
    t, d = xb.shape
    nb = t // tm
    nat = pl.BlockSpec((tm * SB_HEADS, HEAD_DIM), lambda i: (layer * nb + i, 0))
    cols = pl.BlockSpec((tm, SB_WIDTH), lambda i: (i, 0))
    any_spec = pl.BlockSpec(memory_space=pl.ANY)
    return pl.pallas_call(
        _proj_kv_kernel, grid=(nb,),
        in_specs=[pl.BlockSpec((tm, d), lambda i: (i, 0)), _resident((d, 2 * SB_WIDTH)),
                  any_spec, any_spec],
        out_specs=[nat, nat, cols, cols],
        out_shape=[jax.ShapeDtypeStruct(k_all.shape, F32), jax.ShapeDtypeStruct(v_all.shape, F32),
                   jax.ShapeDtypeStruct((t, SB_WIDTH), BF16), jax.ShapeDtypeStruct((t, SB_WIDTH), BF16)],
        input_output_aliases={2: 0, 3: 1},
        compiler_params=_cparams(1), name="proj_kv",
    )(xb, w, k_all, v_all)


def _sb_span(qb, kb, vb, mt2, carry, acc, mask):
    tq, tk = qb.shape[0], kb.shape[0]
    cw = min(SB_BLK, tk)
    nc = tk // cw
    z = _dot_nt(qb, kb) * ATT_SCALE
    ls = _log_sigmoid(z)
    lk = ls - z
    if mask is not None:
        lk = jnp.where(mask, lk, 0.0)
    hi, lo = _split_bf16(lk)
    lhs = jnp.concatenate(
        [jnp.concatenate([hi[:, c * cw:(c + 1) * cw], lo[:, c * cw:(c + 1) * cw]], axis=1)
         for c in range(nc)], axis=0)
    rhs = mt2 if cw == SB_BLK else jnp.concatenate([mt2[:cw], mt2[SB_BLK:SB_BLK + cw]], axis=0)
    lt = _dot(lhs, rhs)
    ws = [None] * nc
    for c in reversed(range(nc)):
        rows = slice(c * tq, (c + 1) * tq)
        cols = slice(c * cw, (c + 1) * cw)
        w = jnp.exp(ls[:, cols] + lt[rows, :cw] + carry[:, :cw])
        if mask is not None:
            w = jnp.where(mask[:, cols], w, 0.0)
        ws[c] = w.astype(BF16)
        carry = carry + lt[rows, SB_BLK:]
    acc = acc + _dot(jnp.concatenate(ws, axis=1), vb)
    return carry, acc


def _strict_lower(n):
    row = lax.broadcasted_iota(jnp.int32, (n, n), 0)
    col = lax.broadcasted_iota(jnp.int32, (n, n), 1)
    return col < row


def _sb_prompt_kernel(q_ref, k_ref, v_ref, mt_ref, o_ref, carry_ref, acc_ref):
    i = pl.program_id(2)
    tq = q_ref.shape[0]
    n_heads = q_ref.shape[1] // HEAD_DIM
    mt2 = mt_ref[...]
    zero = jnp.zeros((tq, SB_BLK), F32)
    mask = _strict_lower(tq)
    diag = pl.ds(pl.multiple_of(i * tq, tq), tq)
    for h in range(n_heads):
        cs = slice(h * HEAD_DIM, (h + 1) * HEAD_DIM)
        carry, acc = _sb_span(q_ref[:, cs], k_ref[diag, cs], v_ref[diag, cs], mt2, zero, zero, mask)
        carry_ref[h] = carry
        acc_ref[h] = acc

    def body(s, c):
        rows = pl.ds(pl.multiple_of((i - 1 - s) * tq, tq), tq)
        for h in range(n_heads):
            cs = slice(h * HEAD_DIM, (h + 1) * HEAD_DIM)
            carry, acc = _sb_span(q_ref[:, cs], k_ref[rows, cs], v_ref[rows, cs], mt2,
                                  carry_ref[h], acc_ref[h], None)
            carry_ref[h] = carry
            acc_ref[h] = acc
        return c

    lax.fori_loop(0, i, body, 0)
    for h in range(n_heads):
        o_ref[:, h * HEAD_DIM:(h + 1) * HEAD_DIM] = acc_ref[h].astype(o_ref.dtype)


def _sb_prompt(qb, kb, vb, mt2, batch, seq):
    tq = min(SB_TQ, seq)
    nq = seq // tq
    hg = SB_HEAD_GROUP
    gw = hg * HEAD_DIM
    q_spec = pl.BlockSpec((tq, gw), lambda b, g, i: (b * nq + i, g))
    kv_spec = pl.BlockSpec((seq, gw), lambda b, g, i: (b, g))
    return pl.pallas_call(
        _sb_prompt_kernel, grid=(batch, SB_HEADS // hg, nq),
        in_specs=[q_spec, kv_spec, kv_spec, _resident((2 * SB_BLK, 2 * SB_BLK))],
        out_specs=q_spec,
        out_shape=jax.ShapeDtypeStruct((batch * seq, SB_WIDTH), BF16),
        scratch_shapes=[pltpu.VMEM((hg, tq, SB_BLK), F32), pltpu.VMEM((hg, tq, HEAD_DIM), F32)],
        compiler_params=_cparams(3), name="sb_prompt",
    )(qb, kb, vb, mt2)


def _sb_sample_kernel(q_ref, kn_ref, vn_ref, kp_ref, vp_ref, mt_ref, o_ref, carry_ref, acc_ref):
    c = pl.program_id(1)
    tq = q_ref.shape[0]
    n_pos = kp_ref.shape[0] // SB_HEADS
    mt2 = mt_ref[...]

    @pl.when(c == 0)
    def _():
        zero = jnp.zeros((tq, SB_BLK), F32)
        mask = _strict_lower(tq)
        for h in range(SB_HEADS):
            cs = slice(h * HEAD_DIM, (h + 1) * HEAD_DIM)
            carry, acc = _sb_span(q_ref[:, cs], kn_ref[:, cs], vn_ref[:, cs], mt2, zero, zero, mask)
            carry_ref[h] = carry
            acc_ref[h] = acc

    for h in range(SB_HEADS):
        cs = slice(h * HEAD_DIM, (h + 1) * HEAD_DIM)
        rows = pl.ds(h, n_pos, stride=SB_HEADS)
        carry, acc = _sb_span(q_ref[:, cs], kp_ref[rows, :].astype(BF16), vp_ref[rows, :].astype(BF16),
                              mt2, carry_ref[h], acc_ref[h], None)
        carry_ref[h] = carry
        acc_ref[h] = acc

    @pl.when(c == pl.num_programs(1) - 1)
    def _():
        for h in range(SB_HEADS):
            o_ref[:, h * HEAD_DIM:(h + 1) * HEAD_DIM] = acc_ref[h].astype(o_ref.dtype)


def _sb_sample(qb, kb, vb, past_k, past_v, mt2, batch, tq, *, chunk):
    past = past_k.shape[1] // SB_HEADS
    n_chunks = past // chunk
    new = pl.BlockSpec((tq, SB_WIDTH), lambda b, c: (b, 0))
    cache = pl.BlockSpec((None, chunk * SB_HEADS, HEAD_DIM), lambda b, c: (b, n_chunks - 1 - c, 0))
    return pl.pallas_call(
        _sb_sample_kernel, grid=(batch, n_chunks),
        in_specs=[new, new, new, cache, cache, _resident((2 * SB_BLK, 2 * SB_BLK))],
        out_specs=new,
        out_shape=jax.ShapeDtypeStruct((batch * tq, SB_WIDTH), BF16),
        scratch_shapes=[pltpu.VMEM((SB_HEADS, tq, SB_BLK), F32),
                        pltpu.VMEM((SB_HEADS, tq, HEAD_DIM), F32)],
        compiler_params=_cparams(2), name="sb_sample",
    )(qb, kb, vb, past_k, past_v, mt2)


def _rec_heads(q, k, v, log_g, gate, norm_g, lbd, st_ref, o_ref):
    tc = q.shape[0]
    hi, lo = _split_bf16(log_g)
    cum = _dot(lbd, hi) + _dot(lbd, lo)
    row = lax.broadcasted_iota(jnp.int32, (REC_BLK, REC_BLK), 0)
    col = lax.broadcasted_iota(jnp.int32, (REC_BLK, REC_BLK), 1)
    causal = col <= row
    half = REC_BLK // 2
    for h in range(REC_HEADS):
        cs = slice(h * HEAD_DIM, (h + 1) * HEAD_DIM)
        for n in range(tc // REC_BLK):
            rs = slice(n * REC_BLK, (n + 1) * REC_BLK)
            c = cum[rs, cs]
            mid = c[half - 1:half, :]
            last = c[REC_BLK - 1:REC_BLK, :]
            q1 = q[rs, cs] * jnp.exp(c - mid)
            k1 = k[rs, cs] * jnp.exp(mid - c)
            vb = v[rs, cs].astype(BF16)
            s = jnp.where(causal, _dot_nt(q1.astype(BF16), k1.astype(BF16)), 0.0)
            o = _dot(s.astype(BF16), vb)
            st = st_ref[h]
            qd = q1 * jnp.exp(mid)
            o = o + _dot_nt(qd.astype(BF16), st.astype(BF16))
            ke = k1 * jnp.exp(last - mid)
            st_ref[h] = st * jnp.exp(last) + _dot_tn(vb, ke.astype(BF16))
            ms = jnp.mean(o * o, axis=-1, keepdims=True)
            o = o * lax.rsqrt(ms + LN_EPS) * norm_g
            g = gate[rs, cs]
            o_ref[rs, cs] = (o * (g * _sigmoid(g))).astype(o_ref.dtype)


def _rec_state_io(s0_ref, sout_ref, st_ref, body):
    c = pl.program_id(1)

    @pl.when(c == 0)
    def _():
        for h in range(REC_HEADS):
            st_ref[h] = s0_ref[0, h].T

    body()

    @pl.when(c == pl.num_programs(1) - 1)
    def _():
        for h in range(REC_HEADS):
            sout_ref[0, h] = st_ref[h].T


def _gla_kernel(q_ref, k_ref, v_ref, g_ref, a_ref, wa_ref, ba_ref, ng_ref, lbd_ref, s0_ref,
                o_ref, sout_ref, st_ref):
    def body():
        pre = _dot(a_ref[...].astype(BF16), wa_ref[...]) + ba_ref[...]
        log_g = _log_sigmoid(pre) * (1.0 / GLA_GATE_NORM)
        _rec_heads(q_ref[...] * ATT_SCALE, k_ref[...], v_ref[...], log_g, g_ref[...],
                   ng_ref[...], lbd_ref[...], st_ref, o_ref)
    _rec_state_io(s0_ref, sout_ref, st_ref, body)


def _hgrn_kernel(f_ref, q_ref, i_ref, g_ref, lb_ref, ng_ref, lbd_ref, s0_ref,
                 o_ref, sout_ref, st_ref):
    def body():
        lb = lb_ref[...]
        zf = f_ref[...]
        a = jnp.log(lb)
        b = jnp.log1p(-lb) + _log_sigmoid(zf)
        log_f = jnp.maximum(a, b) + jnp.log1p(jnp.exp(-jnp.abs(a - b)))
        k = (1.0 - lb) * _sigmoid(-zf)
        hq = q_ref[...]
        q = hq * _sigmoid(hq) * ATT_SCALE
        _rec_heads(q, k, i_ref[...], log_f, g_ref[...], ng_ref[...], lbd_ref[...], st_ref, o_ref)
    _rec_state_io(s0_ref, sout_ref, st_ref, body)


def _rec_specs(batch, t_b, tc):
    nc = t_b // tc
    col = lambda c0: pl.BlockSpec((tc, REC_WIDTH), lambda b, c, c0=c0: (b * nc + c, c0 // REC_WIDTH))
    full = lambda shape: pl.BlockSpec(shape, lambda b, c: (0,) * len(shape))
    state = pl.BlockSpec((1, REC_HEADS, HEAD_DIM, HEAD_DIM), lambda b, c: (b, 0, 0, 0))
    out_specs = [pl.BlockSpec((tc, REC_WIDTH), lambda b, c: (b * nc + c, 0)), state]
    out_shape = [jax.ShapeDtypeStruct((batch * t_b, REC_WIDTH), BF16),
                 jax.ShapeDtypeStruct((batch, REC_HEADS, HEAD_DIM, HEAD_DIM), F32)]
    scratch = [pltpu.VMEM((REC_HEADS, HEAD_DIM, HEAD_DIM), F32)]
    return nc, col, full, state, out_specs, out_shape, scratch


def _gla(rec, wa, ba, ng, lbd, s0, batch, t_b, tc):
    nc, col, full, state, out_specs, out_shape, scratch = _rec_specs(batch, t_b, tc)
    a_spec = pl.BlockSpec((tc, HEAD_DIM), lambda b, c: (b * nc + c, R_GA // HEAD_DIM))
    return pl.pallas_call(
        _gla_kernel, grid=(batch, nc),
        in_specs=[col(R_GQ), col(R_GK), col(R_GV), col(R_GG), a_spec,
                  full((HEAD_DIM, REC_WIDTH)), full((1, REC_WIDTH)), full((1, HEAD_DIM)),
                  full((tc, tc)), state],
        out_specs=out_specs, out_shape=out_shape, scratch_shapes=scratch,
        compiler_params=_cparams(2), name="gla",
    )(rec, rec, rec, rec, rec, wa, ba, ng, lbd, s0)


def _hgrn(rec, lb, ng, lbd, s0, batch, t_b, tc):
    nc, col, full, state, out_specs, out_shape, scratch = _rec_specs(batch, t_b, tc)
    return pl.pallas_call(
        _hgrn_kernel, grid=(batch, nc),
        in_specs=[col(R_HF), col(R_HQ), col(R_HI), col(R_HO),
                  full((1, REC_WIDTH)), full((1, HEAD_DIM)), full((tc, tc)), state],
        out_specs=out_specs, out_shape=out_shape, scratch_shapes=scratch,
        compiler_params=_cparams(2), name="hgrn",
    )(rec, rec, rec, rec, lb, ng, lbd, s0)


def _mix_kernel(x_ref, ga_ref, gb_ref, gc_ref, oa_ref, ob_ref, oc_ref,
                wsb_ref, wgla_ref, whg_ref, wo_ref, g_ref, b_ref, o_ref):
    m = _sigmoid(ga_ref[...]) * _dot(oa_ref[...], wsb_ref[...])
    m = m + _sigmoid(gb_ref[...]) * _dot(ob_ref[...], wgla_ref[...])
    m = m + _sigmoid(gc_ref[...]) * _dot(oc_ref[...], whg_ref[...])
    y = ALPHA * x_ref[...] + _dot(m.astype(BF16), wo_ref[...])
    o_ref[...] = _ln_rows(y, g_ref[...], b_ref[...])


def _mix(x, gates, oa, ob, oc, wsb, wgla, whg, wo, g, b, *, tm):
    t, d = x.shape
    row = lambda w, j=0: pl.BlockSpec((tm, w), lambda i, j=j: (i, j))
    return pl.pallas_call(
        _mix_kernel, grid=(t // tm,),
        in_specs=[row(d), row(d, 0), row(d, 1), row(d, 2),
                  row(SB_WIDTH), row(REC_WIDTH), row(REC_WIDTH),
                  _resident((SB_WIDTH, d)), _resident((REC_WIDTH, d)), _resident((REC_WIDTH, d)),
                  _resident((d, d)), _resident((1, d)), _resident((1, d))],
        out_specs=row(d),
        out_shape=jax.ShapeDtypeStruct((t, d), F32),
        compiler_params=_cparams(1), name="mix",
    )(x, gates, gates, gates, oa, ob, oc, wsb, wgla, whg, wo, g.reshape(1, d), b.reshape(1, d))


def _xattn_kernel(x_ref, mk_ref, mv_ref, wq_ref, wo_ref, g_ref, b_ref, o_ref):
    x = x_ref[...]
    q = _dot(x.astype(BF16), wq_ref[...])
    outs = []
    for h in range(XA_HEADS):
        cs = slice(h * HEAD_DIM, (h + 1) * HEAD_DIM)
        s = _dot_nt(q[:, cs].astype(BF16), mk_ref[:, cs].astype(BF16)) * ATT_SCALE
        s = s - jnp.max(s, axis=-1, keepdims=True)
        e = jnp.exp(s)
        p = e / jnp.sum(e, axis=-1, keepdims=True)
        outs.append(_dot(p.astype(BF16), mv_ref[:, cs].astype(BF16)))
    o = jnp.concatenate(outs, axis=-1)
    y = ALPHA * x + _dot(o.astype(BF16), wo_ref[...])
    o_ref[...] = _ln_rows(y, g_ref[...], b_ref[...])


def _xattn(x, mem_k, mem_v, wq, wo, g, b, *, batch, tm):
    t, d = x.shape
    nt = (t // batch) // tm
    n_mem = mem_k.shape[1]
    row = pl.BlockSpec((tm, d), lambda bi, i: (bi * nt + i, 0))
    mem = pl.BlockSpec((None, n_mem, XA_WIDTH), lambda bi, i: (bi, 0, 0))
    return pl.pallas_call(
        _xattn_kernel, grid=(batch, nt),
        in_specs=[row, mem, mem, _resident((d, XA_WIDTH)), _resident((XA_WIDTH, d)),
                  _resident((1, d)), _resident((1, d))],
        out_specs=row,
        out_shape=jax.ShapeDtypeStruct((t, d), F32),
        compiler_params=_cparams(2), name="xattn",
    )(x, mem_k, mem_v, wq, wo, g.reshape(1, d), b.reshape(1, d))


def _ffn_kernel(x_ref, wg_ref, wu_ref, wd_ref, g_ref, b_ref, o_ref, ob_ref, xs_ref, acc_ref):
    f = pl.program_id(1)

    @pl.when(f == 0)
    def _():
        xs_ref[...] = x_ref[...].astype(BF16)
        acc_ref[...] = jnp.zeros_like(acc_ref)

    xs = xs_ref[...]
    gate = _dot(xs, wg_ref[...])
    up = _dot(xs, wu_ref[...])
    hidden = gate * _sigmoid(gate) * up
    acc_ref[...] += _dot(hidden.astype(BF16), wd_ref[...])

    @pl.when(f == pl.num_programs(1) - 1)
    def _():
        y = _ln_rows(ALPHA * x_ref[...] + acc_ref[...], g_ref[...], b_ref[...])
        o_ref[...] = y
        ob_ref[...] = y.astype(BF16)


def _ffn(x, w_in, w_out, g, b, *, tm, tf):
    t, d = x.shape
    nf = D_FF // tf
    row = pl.BlockSpec((tm, d), lambda i, f: (i, 0))
    vec = pl.BlockSpec((1, d), lambda i, f: (0, 0))
    return pl.pallas_call(
        _ffn_kernel, grid=(t // tm, nf),
        in_specs=[row,
                  pl.BlockSpec((d, tf), lambda i, f: (0, f)),
                  pl.BlockSpec((d, tf), lambda i, f: (0, nf + f)),
                  pl.BlockSpec((tf, d), lambda i, f: (f, 0)),
                  vec, vec],
        out_specs=[row, row],
        out_shape=[jax.ShapeDtypeStruct((t, d), F32), jax.ShapeDtypeStruct((t, d), BF16)],
        scratch_shapes=[pltpu.VMEM((tm, d), BF16), pltpu.VMEM((tm, d), F32)],
        compiler_params=_cparams(2), name="ffn",
    )(x, w_in, w_in, w_out, g.reshape(1, d), b.reshape(1, d))


def _tri_tables():
    j = lax.broadcasted_iota(jnp.int32, (SB_BLK, SB_BLK), 0)
    s = lax.broadcasted_iota(jnp.int32, (SB_BLK, SB_BLK), 1)
    mt = jnp.concatenate([(j > s).astype(BF16), jnp.ones((SB_BLK, SB_BLK), BF16)], axis=1)
    return jnp.concatenate([mt, mt], axis=0)


def _block_cumsum_table(tc):
    r = lax.broadcasted_iota(jnp.int32, (tc, tc), 0)
    c = lax.broadcasted_iota(jnp.int32, (tc, tc), 1)
    return ((c <= r) & (r // REC_BLK == c // REC_BLK)).astype(BF16)


def _layer(x, xb, lw, layer, mem_k, mem_v, s_gla, s_hg, past, ln_in, k_all, v_all, *,
           batch, t_b, cfg):
    tm = cfg["tm_proj"]
    if ln_in is not None:
        gates, x, xb = _proj_ln(x, ln_in[0], ln_in[1], lw["w_gates"], tm=tm // 2, tn=1024)
    else:
        gates = _mm(xb, lw["w_gates"], F32, tm=tm, tn=1024)
    qb = _mm(xb, lw["w_sq"], BF16, tm=tm, tn=SB_WIDTH)
    rec = _mm(xb, lw["w_rec"], F32, tm=tm, tn=N_REC // 3)
    k_all, v_all, kb, vb = _proj_kv(xb, lw["w_skv"], k_all, v_all, layer, tm=cfg["tm_kv"])
    if past is None:
        oa = _sb_prompt(qb, kb, vb, lw["mt2"], batch, t_b)
    else:
        oa = _sb_sample(qb, kb, vb, past[0], past[1], lw["mt2"], batch, t_b, chunk=cfg["sb_chunk"])
    tc = cfg["tc"]
    lbd = _block_cumsum_table(tc)
    ob, s_gla_new = _gla(rec, lw["wa"], lw["ba"], lw["gla_ng"], lbd, s_gla, batch, t_b, tc)
    oc, s_hg_new = _hgrn(rec, lw["lb"], lw["hg_ng"], lbd, s_hg, batch, t_b, tc)
    x = _mix(x, gates, oa, ob, oc, lw["w_up_sb"], lw["w_up_gla"], lw["w_up_hg"], lw["w_o"],
             lw["ln1_g"], lw["ln1_b"], tm=cfg["tm_mix"])
    x = _xattn(x, mem_k, mem_v, lw["w_xq"], lw["w_xo"], lw["ln2_g"], lw["ln2_b"],
               batch=batch, tm=cfg["tm_xa"])
    x, xb = _ffn(x, lw["w_ffn_in"], lw["w_ffn_out"], lw["ln3_g"], lw["ln3_b"],
                 tm=cfg["tm_ffn"], tf=cfg["tf"])
    return x, xb, k_all, v_all, s_gla_new, s_hg_new


def _tile(n, pref):
    t = min(n, pref)
    while n % t:
        t //= 2
    return t


def kernel(x_prompt, x_sample, mem_prompt, cache_sb_k, cache_sb_v, cache_mem_k, cache_mem_v,
           state_gla, state_hgrn, ln_in_g, ln_in_b, w_in, gla_w_a, gla_b_a, gla_norm_g, hg_lower,
           hg_norm_g, w_up_sb, w_up_gla, w_up_hg, w_o, ln1_g, ln1_b, w_xq, w_xk, w_xv, w_xo,
           ln2_g, ln2_b, w_ffn_in, w_ffn_out, ln3_g, ln3_b):
    bp, seq, d = x_prompt.shape
    bs, tq, _ = x_sample.shape
    n_mem = mem_prompt.shape[1]
    past_len = cache_sb_k.shape[2]
    depth = w_in.shape[0]

    lb_all = jnp.cumsum(jax.nn.softmax(hg_lower.astype(F32), axis=0), axis=0)
    lb_all = lb_all - lb_all[0:1]
    mt2 = _tri_tables()
    wa_pad = jnp.zeros((depth, HEAD_DIM, REC_WIDTH), BF16).at[:, :GLA_RANK].set(gla_w_a.astype(BF16))
    ga_zero = jnp.zeros((d, GA_PAD - GLA_RANK), BF16)

    xp = x_prompt.reshape(bp * seq, d)
    xs = x_sample.reshape(bs * tq, d)
    xpb = xsb = None
    mem2d = mem_prompt.reshape(bp * n_mem, d)
    zero_state = jnp.zeros((bp, REC_HEADS, HEAD_DIM, HEAD_DIM), F32)
    kp_all = lax.empty((depth * bp * seq * SB_HEADS, HEAD_DIM), F32)
    vp_all = lax.empty((depth * bp * seq * SB_HEADS, HEAD_DIM), F32)
    ks_all = lax.empty((depth * bs * tq * SB_HEADS, HEAD_DIM), F32)
    vs_all = lax.empty((depth * bs * tq * SB_HEADS, HEAD_DIM), F32)

    cfg_p = dict(tm_proj=_tile(bp * seq, 1024), tm_kv=_tile(bp * seq, 512), tc=_tile(seq, 256),
                 tm_mix=_tile(bp * seq, 256), tm_xa=_tile(seq, 256),
                 tm_ffn=_tile(bp * seq, 512), tf=512)
    cfg_s = dict(tm_proj=_tile(bs * tq, 1024), tm_kv=_tile(bs * tq, 512), tc=_tile(tq, 256),
                 tm_mix=_tile(bs * tq, 256), tm_xa=_tile(tq, 256),
                 tm_ffn=_tile(bs * tq, 512), tf=512, sb_chunk=_tile(past_len, 1024))

    outs_p, outs_s = [], []
    for l in range(depth):
        w = w_in[l]
        lw = dict(
            w_gates=w[:, W_GATES:].astype(BF16), w_sq=w[:, W_SQ:W_SKV].astype(BF16),
            w_skv=w[:, W_SKV:W_REC_A].astype(BF16),
            w_rec=jnp.concatenate([w[:, W_REC_A:W_REC_H].astype(BF16), ga_zero,
                                   w[:, W_REC_H:W_GATES].astype(BF16)], axis=1),
            mt2=mt2, wa=wa_pad[l], ba=gla_b_a[l].reshape(1, REC_WIDTH),
            gla_ng=gla_norm_g[l].reshape(1, HEAD_DIM), lb=lb_all[l].reshape(1, REC_WIDTH),
            hg_ng=hg_norm_g[l].reshape(1, HEAD_DIM),
            w_up_sb=w_up_sb[l].astype(BF16), w_up_gla=w_up_gla[l].astype(BF16),
            w_up_hg=w_up_hg[l].astype(BF16), w_o=w_o[l].astype(BF16),
            ln1_g=ln1_g[l], ln1_b=ln1_b[l], w_xq=w_xq[l].astype(BF16), w_xo=w_xo[l].astype(BF16),
            ln2_g=ln2_g[l], ln2_b=ln2_b[l], w_ffn_in=w_ffn_in[l].astype(BF16),
            w_ffn_out=w_ffn_out[l].astype(BF16), ln3_g=ln3_g[l], ln3_b=ln3_b[l])
        ln_in = (ln_in_g, ln_in_b) if l == 0 else None
        tm_mem = _tile(bp * n_mem, 512)
        mem_k = _mm(mem2d, w_xk[l].astype(BF16), F32, tm=tm_mem, tn=XA_WIDTH)
        mem_v = _mm(mem2d, w_xv[l].astype(BF16), F32, tm=tm_mem, tn=XA_WIDTH)
        xp, xpb, kp_all, vp_all, gp, hp = _layer(
            xp, xpb, lw, l, mem_k.reshape(bp, n_mem, XA_WIDTH), mem_v.reshape(bp, n_mem, XA_WIDTH),
            zero_state, zero_state, None, ln_in, kp_all, vp_all, batch=bp, t_b=seq, cfg=cfg_p)
        past = (cache_sb_k[l].reshape(bs, past_len * SB_HEADS, HEAD_DIM),
                cache_sb_v[l].reshape(bs, past_len * SB_HEADS, HEAD_DIM))
        xs, xsb, ks_all, vs_all, gs, hs = _layer(
            xs, xsb, lw, l, cache_mem_k[l].reshape(bs, n_mem, XA_WIDTH),
            cache_mem_v[l].reshape(bs, n_mem, XA_WIDTH),
            state_gla[l], state_hgrn[l], past, ln_in, ks_all, vs_all, batch=bs, t_b=tq, cfg=cfg_s)
        outs_p.append((mem_k.reshape(bp, n_mem, XA_HEADS, HEAD_DIM),
                       mem_v.reshape(bp, n_mem, XA_HEADS, HEAD_DIM), gp, hp))
        outs_s.append((gs, hs))
    stk = lambda outs, i: jnp.stack([o[i] for o in outs], axis=0)
    kv_p = (depth, bp, seq, SB_HEADS, HEAD_DIM)
    kv_s = (depth, bs, tq, SB_HEADS, HEAD_DIM)
    return (xp.reshape(bp, seq, d), xs.reshape(bs, tq, d),
            kp_all.reshape(kv_p), vp_all.reshape(kv_p), stk(outs_p, 0), stk(outs_p, 1),
            stk(outs_p, 2), stk(outs_p, 3), ks_all.reshape(kv_s), vs_all.reshape(kv_s),
            stk(outs_s, 0), stk(outs_s, 1))
```

```python
import functools

import jax
import jax.numpy as jnp
from jax import lax
from jax.experimental import pallas as pl
from jax.experimental.pallas import tpu as pltpu

F32 = jnp.float32
BF16 = jnp.bfloat16

D_MODEL = 2048
DEPTH = 2
HEAD_DIM = 128
SB_HEADS = 8
SB_WIDTH = SB_HEADS * HEAD_DIM
REC_HEADS = 4
REC_WIDTH = REC_HEADS * HEAD_DIM
GLA_RANK = 16
GLA_GATE_NORM = 16.0
XA_HEADS = 4
XA_WIDTH = XA_HEADS * HEAD_DIM
D_FF = 5632
LN_EPS = 1e-5
ALPHA = (2 * DEPTH) ** 0.25
ATT_SCALE = HEAD_DIM ** -0.5

W_SQ = 0
W_SKV = SB_WIDTH
W_REC_A = 3 * SB_WIDTH
W_GA = W_REC_A + 4 * REC_WIDTH
W_REC_H = W_GA + GLA_RANK
W_GATES = W_REC_H + 4 * REC_WIDTH

GA_PAD = 512
R_GQ, R_GK, R_GV, R_GG, R_GA, R_HF, R_HQ, R_HI, R_HO = (i * REC_WIDTH for i in range(9))
N_REC = 9 * REC_WIDTH

SB_BLK = 128
SB_TQ = 256
SB_HEAD_GROUP = 4
REC_BLK = 64
VMEM_LIMIT = 56 * 1024 * 1024


def _cparams(n_axes):
    return pltpu.CompilerParams(dimension_semantics=("arbitrary",) * n_axes,
                                vmem_limit_bytes=VMEM_LIMIT)


def _ln_rows(x, g, b):
    mu = jnp.mean(x, axis=-1, keepdims=True)
    xc = x - mu
    var = jnp.mean(xc * xc, axis=-1, keepdims=True)
    return xc * lax.rsqrt(var + LN_EPS) * g + b


def _log1p_exp(neg):
    return jnp.log(1.0 + jnp.exp(neg))


def _log_sigmoid(z):
    return jnp.minimum(z, 0.0) - _log1p_exp(-jnp.abs(z))


def _sigmoid(z):
    return 1.0 / (1.0 + jnp.exp(-z))


def _dot(a, b):
    return jnp.dot(a, b, preferred_element_type=F32)


def _dot_nt(a, b):
    return lax.dot_general(a, b, (((1,), (1,)), ((), ())), preferred_element_type=F32)


def _dot_tn(a, b):
    return lax.dot_general(a, b, (((0,), (0,)), ((), ())), preferred_element_type=F32)


def _split_bf16(x):
    hi = x.astype(BF16)
    lo = (x - hi.astype(F32)).astype(BF16)
    return hi, lo


def _resident(shape):
    return pl.BlockSpec(shape, lambda *_: (0,) * len(shape), pipeline_mode=pl.Buffered(1))


def _proj_ln_kernel(x_ref, g_ref, b_ref, w_ref, o_ref, xn_ref, xb_ref, *, row_chunk):
    @pl.when(pl.program_id(1) == 0)
    def _():
        def body(r, carry):
            rows = pl.ds(pl.multiple_of(r * row_chunk, row_chunk), row_chunk)
            x = _ln_rows(x_ref[rows, :], g_ref[...], b_ref[...])
            xn_ref[rows, :] = x
            xb_ref[rows, :] = x.astype(BF16)
            return carry
        lax.fori_loop(0, x_ref.shape[0] // row_chunk, body, 0)

    o_ref[...] = _dot(xb_ref[...], w_ref[...])


def _proj_ln(x, g, b, w, *, tm, tn):
    t, d = x.shape
    n = w.shape[1]
    x_spec = pl.BlockSpec((tm, d), lambda i, j: (i, 0))
    vec_spec = pl.BlockSpec((1, d), lambda i, j: (0, 0))
    return pl.pallas_call(
        functools.partial(_proj_ln_kernel, row_chunk=min(128, tm)), grid=(t // tm, n // tn),
        in_specs=[x_spec, vec_spec, vec_spec, pl.BlockSpec((d, tn), lambda i, j: (0, j))],
        out_specs=[pl.BlockSpec((tm, tn), lambda i, j: (i, j)), x_spec, x_spec],
        out_shape=[jax.ShapeDtypeStruct((t, n), F32), jax.ShapeDtypeStruct((t, d), F32),
                   jax.ShapeDtypeStruct((t, d), BF16)],
        compiler_params=_cparams(2), name="proj_ln",
    )(x, g.reshape(1, d), b.reshape(1, d), w)


def _mm_kernel(x_ref, w_ref, o_ref):
    o_ref[...] = _dot(x_ref[...].astype(BF16), w_ref[...]).astype(o_ref.dtype)


def _mm(x, w, out_dtype, *, tm, tn):
    t, d = x.shape
    n = w.shape[1]
    return pl.pallas_call(
        _mm_kernel, grid=(t // tm, n // tn),
        in_specs=[pl.BlockSpec((tm, d), lambda i, j: (i, 0)),
                  pl.BlockSpec((d, tn), lambda i, j: (0, j))],
        out_specs=pl.BlockSpec((tm, tn), lambda i, j: (i, j)),
        out_shape=jax.ShapeDtypeStruct((t, n), out_dtype),
        compiler_params=_cparams(2), name="mm",
    )(x, w)


def _proj_kv_kernel(x_ref, w_ref, *refs):
    k_nat_ref, v_nat_ref, kb_ref, vb_ref = refs[-4:]
    tm = x_ref.shape[0]
    res = _dot(x_ref[...], w_ref[...])
    kb_ref[...] = res[:, :SB_WIDTH].astype(BF16)
    vb_ref[...] = res[:, SB_WIDTH:].astype(BF16)
    for h in range(SB_HEADS):
        rows = pl.ds(h, tm, stride=SB_HEADS)
        k_nat_ref[rows, :] = res[:, h * HEAD_DIM:(h + 1) * HEAD_DIM]
        v_nat_ref[rows, :] = res[:, SB_WIDTH + h * HEAD_DIM:SB_WIDTH + (h + 1) * HEAD_DIM]


def _proj_kv(xb, w, k_all, v_all, layer, depth, *, tm):
    t, d = xb.shape
    nb = t // tm
    nat = pl.BlockSpec((tm * SB_HEADS, HEAD_DIM), lambda i: (layer * nb + i, 0))
    cols = pl.BlockSpec((tm, SB_WIDTH), lambda i: (i, 0))
    in_specs = [pl.BlockSpec((tm, d), lambda i: (i, 0)), _resident((d, 2 * SB_WIDTH))]
    args = [xb, w]
    aliases = {}
    if k_all is not None:
        in_specs += [pl.BlockSpec(memory_space=pl.ANY)] * 2
        args += [k_all, v_all]
        aliases = {2: 0, 3: 1}
    nat_shape = jax.ShapeDtypeStruct((depth * t * SB_HEADS, HEAD_DIM), F32)
    return pl.pallas_call(
        _proj_kv_kernel, grid=(nb,),
        in_specs=in_specs,
        out_specs=[nat, nat, cols, cols],
        out_shape=[nat_shape, nat_shape,
                   jax.ShapeDtypeStruct((t, SB_WIDTH), BF16), jax.ShapeDtypeStruct((t, SB_WIDTH), BF16)],
        input_output_aliases=aliases,
        compiler_params=_cparams(1), name="proj_kv",
    )(*args)


def _sb_scores(qb, kb, mask, ls_ref, lhs_ref, h):
    tq, tk = qb.shape[0], kb.shape[0]
    cw = min(SB_BLK, tk)
    z = _dot_nt(qb, kb) * ATT_SCALE
    ls = _log_sigmoid(z)
    lk = ls - z
    if mask is not None:
        lk = jnp.where(mask, lk, 0.0)
    hi, lo = _split_bf16(lk)
    ls_ref[h, :, :tk] = ls
    for c in range(tk // cw):
        cols = slice(c * cw, (c + 1) * cw)
        lhs_ref[h, c * tq:(c + 1) * tq, :2 * cw] = jnp.concatenate([hi[:, cols], lo[:, cols]], axis=1)


def _sb_weights(tk, mt2, mask, ls_ref, lhs_ref, w_ref, carry_ref, h, first):
    tq = ls_ref.shape[1]
    cw = min(SB_BLK, tk)
    nc = tk // cw
    rhs = mt2 if cw == SB_BLK else jnp.concatenate([mt2[:cw], mt2[SB_BLK:SB_BLK + cw]], axis=0)
    lt = _dot(lhs_ref[h, :nc * tq, :2 * cw], rhs)
    carry = jnp.zeros((tq, SB_BLK), F32) if first else carry_ref[h]
    for c in reversed(range(nc)):
        rows = slice(c * tq, (c + 1) * tq)
        cols = slice(c * cw, (c + 1) * cw)
        w = jnp.exp(ls_ref[h, :, cols] + lt[rows, :cw] + carry[:, :cw])
        if mask is not None:
            w = jnp.where(mask[:, cols], w, 0.0)
        w_ref[h, :, cols] = w.astype(BF16)
        carry = carry + lt[rows, SB_BLK:]
    carry_ref[h] = carry


def _sb_values(tk, vb, w_ref, acc_ref, h, first):
    pv = _dot(w_ref[h, :, :tk], vb)
    acc_ref[h] = pv if first else acc_ref[h] + pv


def _sb_add_span(q_of, k_of, v_of, n_heads, tk, mt2, mask, first, ls_ref, lhs_ref, w_ref,
                 carry_ref, acc_ref):
    for h in range(n_heads):
        _sb_scores(q_of(h), k_of(h), mask, ls_ref, lhs_ref, h)
    for h in range(n_heads):
        _sb_weights(tk, mt2, mask, ls_ref, lhs_ref, w_ref, carry_ref, h, first)
    for h in range(n_heads):
        _sb_values(tk, v_of(h), w_ref, acc_ref, h, first)


def _strict_lower(n):
    row = lax.broadcasted_iota(jnp.int32, (n, n), 0)
    col = lax.broadcasted_iota(jnp.int32, (n, n), 1)
    return col < row


def _sb_scratch(n_heads, tq, tk):
    return [pltpu.VMEM((n_heads, tq, SB_BLK), F32),
            pltpu.VMEM((n_heads, tq, HEAD_DIM), F32),
            pltpu.VMEM((n_heads, tq, tk), F32),
            pltpu.VMEM((n_heads, (tk // SB_BLK) * tq, 2 * SB_BLK), BF16),
            pltpu.VMEM((n_heads, tq, tk), BF16)]


def _sb_prompt_kernel(q_ref, k_ref, v_ref, mt_ref, o_ref, carry_ref, acc_ref, ls_ref, lhs_ref, w_ref):
    i = pl.program_id(2)
    tq = q_ref.shape[0]
    n_heads = q_ref.shape[1] // HEAD_DIM
    mt2 = mt_ref[...]
    heads = [slice(h * HEAD_DIM, (h + 1) * HEAD_DIM) for h in range(n_heads)]

    def add_span(rows, mask, first):
        _sb_add_span(lambda h: q_ref[:, heads[h]], lambda h: k_ref[rows, heads[h]],
                     lambda h: v_ref[rows, heads[h]], n_heads, tq, mt2, mask, first,
                     ls_ref, lhs_ref, w_ref, carry_ref, acc_ref)

    add_span(pl.ds(pl.multiple_of(i * tq, tq), tq), _strict_lower(tq), True)

    def body(s, c):
        add_span(pl.ds(pl.multiple_of((i - 1 - s) * tq, tq), tq), None, False)
        return c

    lax.fori_loop(0, i, body, 0)
    for h in range(n_heads):
        o_ref[:, heads[h]] = acc_ref[h].astype(o_ref.dtype)


def _sb_prompt(qb, kb, vb, mt2, batch, seq):
    tq = min(SB_TQ, seq)
    nq = seq // tq
    hg = SB_HEAD_GROUP
    gw = hg * HEAD_DIM
    q_spec = pl.BlockSpec((tq, gw), lambda b, g, i: (b * nq + i, g))
    kv_spec = pl.BlockSpec((seq, gw), lambda b, g, i: (b, g))
    return pl.pallas_call(
        _sb_prompt_kernel, grid=(batch, SB_HEADS // hg, nq),
        in_specs=[q_spec, kv_spec, kv_spec, _resident((2 * SB_BLK, 2 * SB_BLK))],
        out_specs=q_spec,
        out_shape=jax.ShapeDtypeStruct((batch * seq, SB_WIDTH), BF16),
        scratch_shapes=_sb_scratch(hg, tq, tq),
        compiler_params=_cparams(3), name="sb_prompt",
    )(qb, kb, vb, mt2)


def _sb_sample_kernel(q_ref, kn_ref, vn_ref, kp_ref, vp_ref, mt_ref, o_ref,
                      carry_ref, acc_ref, ls_ref, lhs_ref, w_ref):
    c = pl.program_id(1)
    tq = q_ref.shape[0]
    n_pos = kp_ref.shape[0] // SB_HEADS
    mt2 = mt_ref[...]
    heads = [slice(h * HEAD_DIM, (h + 1) * HEAD_DIM) for h in range(SB_HEADS)]
    scratch = (ls_ref, lhs_ref, w_ref, carry_ref, acc_ref)

    @pl.when(c == 0)
    def _():
        _sb_add_span(lambda h: q_ref[:, heads[h]], lambda h: kn_ref[:, heads[h]],
                     lambda h: vn_ref[:, heads[h]], SB_HEADS, tq, mt2, _strict_lower(tq), True,
                     *scratch)

    past_rows = lambda h: pl.ds(h, n_pos, stride=SB_HEADS)
    _sb_add_span(lambda h: q_ref[:, heads[h]], lambda h: kp_ref[past_rows(h), :].astype(BF16),
                 lambda h: vp_ref[past_rows(h), :].astype(BF16), SB_HEADS, n_pos, mt2, None, False,
                 *scratch)

    @pl.when(c == pl.num_programs(1) - 1)
    def _():
        for h in range(SB_HEADS):
            o_ref[:, heads[h]] = acc_ref[h].astype(o_ref.dtype)


def _sb_sample(qb, kb, vb, past_k, past_v, layer, mt2, batch, tq, *, chunk):
    past = past_k.shape[2] // SB_HEADS
    n_chunks = past // chunk
    new = pl.BlockSpec((tq, SB_WIDTH), lambda b, c: (b, 0))
    cache = pl.BlockSpec((None, None, chunk * SB_HEADS, HEAD_DIM),
                         lambda b, c: (layer, b, n_chunks - 1 - c, 0))
    return pl.pallas_call(
        _sb_sample_kernel, grid=(batch, n_chunks),
        in_specs=[new, new, new, cache, cache, _resident((2 * SB_BLK, 2 * SB_BLK))],
        out_specs=new,
        out_shape=jax.ShapeDtypeStruct((batch * tq, SB_WIDTH), BF16),
        scratch_shapes=_sb_scratch(SB_HEADS, tq, chunk),
        compiler_params=_cparams(2), name="sb_sample",
    )(qb, kb, vb, past_k, past_v, mt2)


def _rec_heads(q, k, v, log_g, gate, norm_g, lbd, st_ref, o_ref):
    tc = q.shape[0]
    hi, lo = _split_bf16(log_g)
    cum = _dot(lbd, hi) + _dot(lbd, lo)
    row = lax.broadcasted_iota(jnp.int32, (REC_BLK, REC_BLK), 0)
    col = lax.broadcasted_iota(jnp.int32, (REC_BLK, REC_BLK), 1)
    causal = col <= row
    half = REC_BLK // 2
    for h in range(REC_HEADS):
        cs = slice(h * HEAD_DIM, (h + 1) * HEAD_DIM)
        for n in range(tc // REC_BLK):
            rs = slice(n * REC_BLK, (n + 1) * REC_BLK)
            c = cum[rs, cs]
            mid = c[half - 1:half, :]
            last = c[REC_BLK - 1:REC_BLK, :]
            q1 = q[rs, cs] * jnp.exp(c - mid)
            k1 = k[rs, cs] * jnp.exp(mid - c)
            vb = v[rs, cs].astype(BF16)
            s = jnp.where(causal, _dot_nt(q1.astype(BF16), k1.astype(BF16)), 0.0)
            o = _dot(s.astype(BF16), vb)
            st = st_ref[h]
            qd = q1 * jnp.exp(mid)
            o = o + _dot_nt(qd.astype(BF16), st.astype(BF16))
            ke = k1 * jnp.exp(last - mid)
            st_ref[h] = st * jnp.exp(last) + _dot_tn(vb, ke.astype(BF16))
            ms = jnp.mean(o * o, axis=-1, keepdims=True)
            o = o * lax.rsqrt(ms + LN_EPS) * norm_g
            g = gate[rs, cs]
            o_ref[rs, cs] = (o * (g * _sigmoid(g))).astype(o_ref.dtype)


def _rec_state_io(s0_ref, sout_ref, st_ref, body):
    c = pl.program_id(1)

    @pl.when(c == 0)
    def _():
        for h in range(REC_HEADS):
            st_ref[h] = s0_ref[0, h].T

    body()

    @pl.when(c == pl.num_programs(1) - 1)
    def _():
        for h in range(REC_HEADS):
            sout_ref[0, h] = st_ref[h].T


def _gla_kernel(q_ref, k_ref, v_ref, g_ref, a_ref, wa_ref, ba_ref, ng_ref, lbd_ref, s0_ref,
                o_ref, sout_ref, st_ref):
    def body():
        pre = _dot(a_ref[...].astype(BF16), wa_ref[...]) + ba_ref[...]
        log_g = _log_sigmoid(pre) * (1.0 / GLA_GATE_NORM)
        _rec_heads(q_ref[...] * ATT_SCALE, k_ref[...], v_ref[...], log_g, g_ref[...],
                   ng_ref[...], lbd_ref[...], st_ref, o_ref)
    _rec_state_io(s0_ref, sout_ref, st_ref, body)


def _hgrn_kernel(f_ref, q_ref, i_ref, g_ref, lb_ref, ng_ref, lbd_ref, s0_ref,
                 o_ref, sout_ref, st_ref):
    def body():
        lb = lb_ref[...]
        zf = f_ref[...]
        a = jnp.log(lb)
        b = jnp.log(1.0 - lb) + _log_sigmoid(zf)
        log_f = jnp.maximum(a, b) + _log1p_exp(-jnp.abs(a - b))
        k = (1.0 - lb) * _sigmoid(-zf)
        hq = q_ref[...]
        q = hq * _sigmoid(hq) * ATT_SCALE
        _rec_heads(q, k, i_ref[...], log_f, g_ref[...], ng_ref[...], lbd_ref[...], st_ref, o_ref)
    _rec_state_io(s0_ref, sout_ref, st_ref, body)


def _rec_specs(batch, t_b, tc, s_layer):
    nc = t_b // tc
    col = lambda c0: pl.BlockSpec((tc, REC_WIDTH), lambda b, c, c0=c0: (b * nc + c, c0 // REC_WIDTH))
    full = lambda shape: pl.BlockSpec(shape, lambda b, c: (0,) * len(shape))
    state = pl.BlockSpec((1, REC_HEADS, HEAD_DIM, HEAD_DIM), lambda b, c: (b, 0, 0, 0))
    state_in = pl.BlockSpec((None, 1, REC_HEADS, HEAD_DIM, HEAD_DIM),
                            lambda b, c: (s_layer, b, 0, 0, 0))
    out_specs = [pl.BlockSpec((tc, REC_WIDTH), lambda b, c: (b * nc + c, 0)), state]
    out_shape = [jax.ShapeDtypeStruct((batch * t_b, REC_WIDTH), BF16),
                 jax.ShapeDtypeStruct((batch, REC_HEADS, HEAD_DIM, HEAD_DIM), F32)]
    scratch = [pltpu.VMEM((REC_HEADS, HEAD_DIM, HEAD_DIM), F32)]
    return nc, col, full, state_in, out_specs, out_shape, scratch


def _gla(rec, wa, ba, ng, lbd, s0, s_layer, batch, t_b, tc):
    nc, col, full, state, out_specs, out_shape, scratch = _rec_specs(batch, t_b, tc, s_layer)
    a_spec = pl.BlockSpec((tc, HEAD_DIM), lambda b, c: (b * nc + c, R_GA // HEAD_DIM))
    return pl.pallas_call(
        _gla_kernel, grid=(batch, nc),
        in_specs=[col(R_GQ), col(R_GK), col(R_GV), col(R_GG), a_spec,
                  full((HEAD_DIM, REC_WIDTH)), full((1, REC_WIDTH)), full((1, HEAD_DIM)),
                  full((tc, tc)), state],
        out_specs=out_specs, out_shape=out_shape, scratch_shapes=scratch,
        compiler_params=_cparams(2), name="gla",
    )(rec, rec, rec, rec, rec, wa, ba, ng, lbd, s0)


def _hgrn(rec, lb, ng, lbd, s0, s_layer, batch, t_b, tc):
    nc, col, full, state, out_specs, out_shape, scratch = _rec_specs(batch, t_b, tc, s_layer)
    return pl.pallas_call(
        _hgrn_kernel, grid=(batch, nc),
        in_specs=[col(R_HF), col(R_HQ), col(R_HI), col(R_HO),
                  full((1, REC_WIDTH)), full((1, HEAD_DIM)), full((tc, tc)), state],
        out_specs=out_specs, out_shape=out_shape, scratch_shapes=scratch,
        compiler_params=_cparams(2), name="hgrn",
    )(rec, rec, rec, rec, lb, ng, lbd, s0)


def _mix_kernel(x_ref, ga_ref, gb_ref, gc_ref, oa_ref, ob_ref, oc_ref,
                wsb_ref, wgla_ref, whg_ref, wo_ref, g_ref, b_ref, o_ref):
    m = _sigmoid(ga_ref[...]) * _dot(oa_ref[...], wsb_ref[...])
    m = m + _sigmoid(gb_ref[...]) * _dot(ob_ref[...], wgla_ref[...])
    m = m + _sigmoid(gc_ref[...]) * _dot(oc_ref[...], whg_ref[...])
    y = ALPHA * x_ref[...] + _dot(m.astype(BF16), wo_ref[...])
    o_ref[...] = _ln_rows(y, g_ref[...], b_ref[...])


def _mix(x, gates, oa, ob, oc, wsb, wgla, whg, wo, g, b, *, tm):
    t, d = x.shape
    row = lambda w, j=0: pl.BlockSpec((tm, w), lambda i, j=j: (i, j))
    return pl.pallas_call(
        _mix_kernel, grid=(t // tm,),
        in_specs=[row(d), row(d, 0), row(d, 1), row(d, 2),
                  row(SB_WIDTH), row(REC_WIDTH), row(REC_WIDTH),
                  _resident((SB_WIDTH, d)), _resident((REC_WIDTH, d)), _resident((REC_WIDTH, d)),
                  _resident((d, d)), _resident((1, d)), _resident((1, d))],
        out_specs=row(d),
        out_shape=jax.ShapeDtypeStruct((t, d), F32),
        compiler_params=_cparams(1), name="mix",
    )(x, gates, gates, gates, oa, ob, oc, wsb, wgla, whg, wo, g.reshape(1, d), b.reshape(1, d))


def _xattn_kernel(x_ref, mk_ref, mv_ref, wq_ref, wo_ref, g_ref, b_ref, o_ref):
    x = x_ref[...]
    q = _dot(x.astype(BF16), wq_ref[...])
    outs = []
    for h in range(XA_HEADS):
        cs = slice(h * HEAD_DIM, (h + 1) * HEAD_DIM)
        s = _dot_nt(q[:, cs].astype(BF16), mk_ref[:, cs].astype(BF16)) * ATT_SCALE
        s = s - jnp.max(s, axis=-1, keepdims=True)
        e = jnp.exp(s)
        p = e / jnp.sum(e, axis=-1, keepdims=True)
        outs.append(_dot(p.astype(BF16), mv_ref[:, cs].astype(BF16)))
    o = jnp.concatenate(outs, axis=-1)
    y = ALPHA * x + _dot(o.astype(BF16), wo_ref[...])
    o_ref[...] = _ln_rows(y, g_ref[...], b_ref[...])


def _xattn(x, mem_k, mem_v, wq, wo, g, b, *, batch, tm):
    t, d = x.shape
    nt = (t // batch) // tm
    n_mem = mem_k.shape[1]
    row = pl.BlockSpec((tm, d), lambda bi, i: (bi * nt + i, 0))
    mem = pl.BlockSpec((None, n_mem, XA_WIDTH), lambda bi, i: (bi, 0, 0))
    return pl.pallas_call(
        _xattn_kernel, grid=(batch, nt),
        in_specs=[row, mem, mem, _resident((d, XA_WIDTH)), _resident((XA_WIDTH, d)),
                  _resident((1, d)), _resident((1, d))],
        out_specs=row,
        out_shape=jax.ShapeDtypeStruct((t, d), F32),
        compiler_params=_cparams(2), name="xattn",
    )(x, mem_k, mem_v, wq, wo, g.reshape(1, d), b.reshape(1, d))


def _ffn_kernel(x_ref, wg_ref, wu_ref, wd_ref, g_ref, b_ref, o_ref, ob_ref, xs_ref, acc_ref):
    f = pl.program_id(1)

    @pl.when(f == 0)
    def _():
        xs_ref[...] = x_ref[...].astype(BF16)
        acc_ref[...] = jnp.zeros_like(acc_ref)

    xs = xs_ref[...]
    gate = _dot(xs, wg_ref[...])
    up = _dot(xs, wu_ref[...])
    hidden = gate * _sigmoid(gate) * up
    acc_ref[...] += _dot(hidden.astype(BF16), wd_ref[...])

    @pl.when(f == pl.num_programs(1) - 1)
    def _():
        y = _ln_rows(ALPHA * x_ref[...] + acc_ref[...], g_ref[...], b_ref[...])
        o_ref[...] = y
        ob_ref[...] = y.astype(BF16)


def _ffn(x, w_in, w_out, g, b, *, tm, tf):
    t, d = x.shape
    nf = D_FF // tf
    row = pl.BlockSpec((tm, d), lambda i, f: (i, 0))
    vec = pl.BlockSpec((1, d), lambda i, f: (0, 0))
    return pl.pallas_call(
        _ffn_kernel, grid=(t // tm, nf),
        in_specs=[row,
                  pl.BlockSpec((d, tf), lambda i, f: (0, f)),
                  pl.BlockSpec((d, tf), lambda i, f: (0, nf + f)),
                  pl.BlockSpec((tf, d), lambda i, f: (f, 0)),
                  vec, vec],
        out_specs=[row, row],
        out_shape=[jax.ShapeDtypeStruct((t, d), F32), jax.ShapeDtypeStruct((t, d), BF16)],
        scratch_shapes=[pltpu.VMEM((tm, d), BF16), pltpu.VMEM((tm, d), F32)],
        compiler_params=_cparams(2), name="ffn",
    )(x, w_in, w_in, w_out, g.reshape(1, d), b.reshape(1, d))


def _tri_tables():
    j = lax.broadcasted_iota(jnp.int32, (SB_BLK, SB_BLK), 0)
    s = lax.broadcasted_iota(jnp.int32, (SB_BLK, SB_BLK), 1)
    mt = jnp.concatenate([(j > s).astype(BF16), jnp.ones((SB_BLK, SB_BLK), BF16)], axis=1)
    return jnp.concatenate([mt, mt], axis=0)


def _block_cumsum_table(tc):
    r = lax.broadcasted_iota(jnp.int32, (tc, tc), 0)
    c = lax.broadcasted_iota(jnp.int32, (tc, tc), 1)
    return ((c <= r) & (r // REC_BLK == c // REC_BLK)).astype(BF16)


def _layer(x, xb, lw, layer, mem_k, mem_v, s_gla, s_hg, s_layer, past, ln_in, k_all, v_all, depth,
           *, batch, t_b, cfg):
    tm = cfg["tm_proj"]
    if ln_in is not None:
        gates, x, xb = _proj_ln(x, ln_in[0], ln_in[1], lw["w_gates"], tm=tm // 2, tn=1024)
    else:
        gates = _mm(xb, lw["w_gates"], F32, tm=tm, tn=1024)
    qb = _mm(xb, lw["w_sq"], BF16, tm=tm, tn=SB_WIDTH)
    rec = _mm(xb, lw["w_rec"], F32, tm=tm, tn=N_REC // 3)
    k_all, v_all, kb, vb = _proj_kv(xb, lw["w_skv"], k_all, v_all, layer, depth, tm=cfg["tm_kv"])
    if past is None:
        oa = _sb_prompt(qb, kb, vb, lw["mt2"], batch, t_b)
    else:
        oa = _sb_sample(qb, kb, vb, past[0], past[1], layer, lw["mt2"], batch, t_b,
                        chunk=cfg["sb_chunk"])
    tc = cfg["tc"]
    lbd = _block_cumsum_table(tc)
    ob, s_gla_new = _gla(rec, lw["wa"], lw["ba"], lw["gla_ng"], lbd, s_gla, s_layer, batch, t_b, tc)
    oc, s_hg_new = _hgrn(rec, lw["lb"], lw["hg_ng"], lbd, s_hg, s_layer, batch, t_b, tc)
    x = _mix(x, gates, oa, ob, oc, lw["w_up_sb"], lw["w_up_gla"], lw["w_up_hg"], lw["w_o"],
             lw["ln1_g"], lw["ln1_b"], tm=cfg["tm_mix"])
    x = _xattn(x, mem_k, mem_v, lw["w_xq"], lw["w_xo"], lw["ln2_g"], lw["ln2_b"],
               batch=batch, tm=cfg["tm_xa"])
    x, xb = _ffn(x, lw["w_ffn_in"], lw["w_ffn_out"], lw["ln3_g"], lw["ln3_b"],
                 tm=cfg["tm_ffn"], tf=cfg["tf"])
    return x, xb, k_all, v_all, s_gla_new, s_hg_new


def _tile(n, pref):
    t = min(n, pref)
    while n % t:
        t //= 2
    return t


def kernel(x_prompt, x_sample, mem_prompt, cache_sb_k, cache_sb_v, cache_mem_k, cache_mem_v,
           state_gla, state_hgrn, ln_in_g, ln_in_b, w_in, gla_w_a, gla_b_a, gla_norm_g, hg_lower,
           hg_norm_g, w_up_sb, w_up_gla, w_up_hg, w_o, ln1_g, ln1_b, w_xq, w_xk, w_xv, w_xo,
           ln2_g, ln2_b, w_ffn_in, w_ffn_out, ln3_g, ln3_b):
    bp, seq, d = x_prompt.shape
    bs, tq, _ = x_sample.shape
    n_mem = mem_prompt.shape[1]
    past_len = cache_sb_k.shape[2]
    depth = w_in.shape[0]

    lb_all = jnp.cumsum(jax.nn.softmax(hg_lower.astype(F32), axis=0), axis=0)
    lb_all = lb_all - lb_all[0:1]
    mt2 = _tri_tables()
    wa_pad = jnp.zeros((depth, HEAD_DIM, REC_WIDTH), BF16).at[:, :GLA_RANK].set(gla_w_a.astype(BF16))
    ga_zero = jnp.zeros((d, GA_PAD - GLA_RANK), BF16)

    xp = x_prompt.reshape(bp * seq, d)
    xs = x_sample.reshape(bs * tq, d)
    xpb = xsb = None
    mem2d = mem_prompt.reshape(bp * n_mem, d)
    zero_state = jnp.zeros((1, bp, REC_HEADS, HEAD_DIM, HEAD_DIM), F32)
    kp_all = vp_all = ks_all = vs_all = None
    past = (cache_sb_k.reshape(depth, bs, past_len * SB_HEADS, HEAD_DIM),
            cache_sb_v.reshape(depth, bs, past_len * SB_HEADS, HEAD_DIM))

    cfg_p = dict(tm_proj=_tile(bp * seq, 1024), tm_kv=_tile(bp * seq, 512), tc=_tile(seq, 256),
                 tm_mix=_tile(bp * seq, 256), tm_xa=_tile(seq, 256),
                 tm_ffn=_tile(bp * seq, 512), tf=512)
    cfg_s = dict(tm_proj=_tile(bs * tq, 1024), tm_kv=_tile(bs * tq, 512), tc=_tile(tq, 256),
                 tm_mix=_tile(bs * tq, 256), tm_xa=_tile(tq, 256),
                 tm_ffn=_tile(bs * tq, 512), tf=512, sb_chunk=_tile(past_len, 1024))

    outs_p, outs_s = [], []
    for l in range(depth):
        w = w_in[l]
        lw = dict(
            w_gates=w[:, W_GATES:].astype(BF16), w_sq=w[:, W_SQ:W_SKV].astype(BF16),
            w_skv=w[:, W_SKV:W_REC_A].astype(BF16),
            w_rec=jnp.concatenate([w[:, W_REC_A:W_REC_H].astype(BF16), ga_zero,
                                   w[:, W_REC_H:W_GATES].astype(BF16)], axis=1),
            mt2=mt2, wa=wa_pad[l], ba=gla_b_a[l].reshape(1, REC_WIDTH),
            gla_ng=gla_norm_g[l].reshape(1, HEAD_DIM), lb=lb_all[l].reshape(1, REC_WIDTH),
            hg_ng=hg_norm_g[l].reshape(1, HEAD_DIM),
            w_up_sb=w_up_sb[l].astype(BF16), w_up_gla=w_up_gla[l].astype(BF16),
            w_up_hg=w_up_hg[l].astype(BF16), w_o=w_o[l].astype(BF16),
            ln1_g=ln1_g[l], ln1_b=ln1_b[l], w_xq=w_xq[l].astype(BF16), w_xo=w_xo[l].astype(BF16),
            ln2_g=ln2_g[l], ln2_b=ln2_b[l], w_ffn_in=w_ffn_in[l].astype(BF16),
            w_ffn_out=w_ffn_out[l].astype(BF16), ln3_g=ln3_g[l], ln3_b=ln3_b[l])
        ln_in = (ln_in_g, ln_in_b) if l == 0 else None
        tm_mem = _tile(bp * n_mem, 512)
        mem_k = _mm(mem2d, w_xk[l].astype(BF16), F32, tm=tm_mem, tn=XA_WIDTH)
        mem_v = _mm(mem2d, w_xv[l].astype(BF16), F32, tm=tm_mem, tn=XA_WIDTH)
        xp, xpb, kp_all, vp_all, gp, hp = _layer(
            xp, xpb, lw, l, mem_k.reshape(bp, n_mem, XA_WIDTH), mem_v.reshape(bp, n_mem, XA_WIDTH),
            zero_state, zero_state, 0, None, ln_in, kp_all, vp_all, depth,
            batch=bp, t_b=seq, cfg=cfg_p)
        xs, xsb, ks_all, vs_all, gs, hs = _layer(
            xs, xsb, lw, l, cache_mem_k[l].reshape(bs, n_mem, XA_WIDTH),
            cache_mem_v[l].reshape(bs, n_mem, XA_WIDTH),
            state_gla, state_hgrn, l, past, ln_in, ks_all, vs_all, depth,
            batch=bs, t_b=tq, cfg=cfg_s)
        outs_p.append((mem_k.reshape(bp, n_mem, XA_HEADS, HEAD_DIM),
                       mem_v.reshape(bp, n_mem, XA_HEADS, HEAD_DIM), gp, hp))
        outs_s.append((gs, hs))
    stk = lambda outs, i: jnp.stack([o[i] for o in outs], axis=0)
    kv_p = (depth, bp, seq, SB_HEADS, HEAD_DIM)
    kv_s = (depth, bs, tq, SB_HEADS, HEAD_DIM)
    return (xp.reshape(bp, seq, d), xs.reshape(bs, tq, d),
            kp_all.reshape(kv_p), vp_all.reshape(kv_p), stk(outs_p, 0), stk(outs_p, 1),
            stk(outs_p, 2), stk(outs_p, 3), ks_all.reshape(kv_s), vs_all.reshape(kv_s),
            stk(outs_s, 0), stk(outs_s, 1))
```

```python
import jax
import jax.numpy as jnp
from jax import lax
from jax.experimental import pallas as pl
from jax.experimental.pallas import tpu as pltpu

F32 = jnp.float32
BF16 = jnp.bfloat16

D_MODEL = 2048
DEPTH = 2
HEAD_DIM = 128
SB_HEADS = 8
SB_WIDTH = SB_HEADS * HEAD_DIM
REC_HEADS = 4
REC_WIDTH = REC_HEADS * HEAD_DIM
GLA_RANK = 16
GLA_GATE_NORM = 16.0
XA_HEADS = 4
XA_WIDTH = XA_HEADS * HEAD_DIM
D_FF = 5632
LN_EPS = 1e-5
ALPHA = (2 * DEPTH) ** 0.25
ATT_SCALE = HEAD_DIM ** -0.5

W_SQ = 0
W_SKV = SB_WIDTH
W_REC_A = 3 * SB_WIDTH
W_GA = W_REC_A + 4 * REC_WIDTH
W_REC_H = W_GA + GLA_RANK
W_GATES = W_REC_H + 4 * REC_WIDTH

GA_PAD = HEAD_DIM
R_GQ, R_GK, R_GV, R_GG, R_HF, R_HQ, R_HI, R_HO, R_GA = (i * REC_WIDTH for i in range(9))
N_REC = R_GA + GA_PAD

SB_BLK = 128
SB_TQ = 256
SB_HEAD_GROUP = 4
REC_BLK = 64
VMEM_LIMIT = 56 * 1024 * 1024


def _cparams(n_axes):
    return pltpu.CompilerParams(dimension_semantics=("arbitrary",) * n_axes,
                                vmem_limit_bytes=VMEM_LIMIT)


def _ln_rows(x, g, b):
    mu = jnp.mean(x, axis=-1, keepdims=True)
    xc = x - mu
    var = jnp.mean(xc * xc, axis=-1, keepdims=True)
    return xc * lax.rsqrt(var + LN_EPS) * g + b


def _log1p_exp(neg):
    return jnp.log(1.0 + jnp.exp(neg))


def _log_sigmoid(z):
    return jnp.minimum(z, 0.0) - _log1p_exp(-jnp.abs(z))


def _sigmoid(z):
    return 1.0 / (1.0 + jnp.exp(-z))


def _dot(a, b):
    return jnp.dot(a, b, preferred_element_type=F32)


def _dot_nt(a, b):
    return lax.dot_general(a, b, (((1,), (1,)), ((), ())), preferred_element_type=F32)


def _dot_tn(a, b):
    return lax.dot_general(a, b, (((0,), (0,)), ((), ())), preferred_element_type=F32)


def _split_bf16(x):
    hi = x.astype(BF16)
    lo = (x - hi.astype(F32)).astype(BF16)
    return hi, lo


def _resident(shape):
    return pl.BlockSpec(shape, lambda *_: (0,) * len(shape), pipeline_mode=pl.Buffered(1))


def _ln_in_kernel(x_ref, g_ref, b_ref, xn_ref, xb_ref):
    x = _ln_rows(x_ref[...], g_ref[...], b_ref[...])
    xn_ref[...] = x
    xb_ref[...] = x.astype(BF16)


def _ln_in(x, g, b, *, tm):
    t, d = x.shape
    x_spec = pl.BlockSpec((tm, d), lambda i: (i, 0))
    vec_spec = pl.BlockSpec((1, d), lambda i: (0, 0))
    return pl.pallas_call(
        _ln_in_kernel, grid=(t // tm,),
        in_specs=[x_spec, vec_spec, vec_spec], out_specs=[x_spec, x_spec],
        out_shape=[jax.ShapeDtypeStruct((t, d), F32), jax.ShapeDtypeStruct((t, d), BF16)],
        compiler_params=_cparams(1), name="ln_in",
    )(x, g.reshape(1, d), b.reshape(1, d))


def _mm_kernel(x_ref, w_ref, o_ref):
    o_ref[...] = _dot(x_ref[...].astype(BF16), w_ref[...]).astype(o_ref.dtype)


def _mm(x, w, out_dtype, *, tm, tn):
    t, d = x.shape
    n = w.shape[1]
    return pl.pallas_call(
        _mm_kernel, grid=(t // tm, n // tn),
        in_specs=[pl.BlockSpec((tm, d), lambda i, j: (i, 0)),
                  pl.BlockSpec((d, tn), lambda i, j: (0, j))],
        out_specs=pl.BlockSpec((tm, tn), lambda i, j: (i, j)),
        out_shape=jax.ShapeDtypeStruct((t, n), out_dtype),
        compiler_params=_cparams(2), name="mm",
    )(x, w)


def _proj_kv_kernel(x_ref, w_ref, *refs):
    k_nat_ref, v_nat_ref, kb_ref, vb_ref = refs[-4:]
    tm = x_ref.shape[0]
    res = _dot(x_ref[...], w_ref[...])
    kb_ref[...] = res[:, :SB_WIDTH].astype(BF16)
    vb_ref[...] = res[:, SB_WIDTH:].astype(BF16)
    for h in range(SB_HEADS):
        rows = pl.ds(h, tm, stride=SB_HEADS)
        k_nat_ref[rows, :] = res[:, h * HEAD_DIM:(h + 1) * HEAD_DIM]
        v_nat_ref[rows, :] = res[:, SB_WIDTH + h * HEAD_DIM:SB_WIDTH + (h + 1) * HEAD_DIM]


def _proj_kv(xb, w, k_all, v_all, layer, depth, *, tm):
    t, d = xb.shape
    nb = t // tm
    nat = pl.BlockSpec((tm * SB_HEADS, HEAD_DIM), lambda i: (layer * nb + i, 0))
    cols = pl.BlockSpec((tm, SB_WIDTH), lambda i: (i, 0))
    in_specs = [pl.BlockSpec((tm, d), lambda i: (i, 0)), _resident((d, 2 * SB_WIDTH))]
    args = [xb, w]
    aliases = {}
    if k_all is not None:
        in_specs += [pl.BlockSpec(memory_space=pl.ANY)] * 2
        args += [k_all, v_all]
        aliases = {2: 0, 3: 1}
    nat_shape = jax.ShapeDtypeStruct((depth * t * SB_HEADS, HEAD_DIM), F32)
    return pl.pallas_call(
        _proj_kv_kernel, grid=(nb,),
        in_specs=in_specs,
        out_specs=[nat, nat, cols, cols],
        out_shape=[nat_shape, nat_shape,
                   jax.ShapeDtypeStruct((t, SB_WIDTH), BF16), jax.ShapeDtypeStruct((t, SB_WIDTH), BF16)],
        input_output_aliases=aliases,
        compiler_params=_cparams(1), name="proj_kv",
    )(*args)


def _sb_scores(qb, kb, mask, ls_ref, lhs_ref, h):
    tq, tk = qb.shape[0], kb.shape[0]
    cw = min(SB_BLK, tk)
    z = _dot_nt(qb, kb) * ATT_SCALE
    ls = _log_sigmoid(z)
    lk = ls - z
    if mask is not None:
        lk = jnp.where(mask, lk, 0.0)
    hi, lo = _split_bf16(lk)
    ls_ref[h, :, :tk] = ls
    for c in range(tk // cw):
        cols = slice(c * cw, (c + 1) * cw)
        lhs_ref[h, c * tq:(c + 1) * tq, :2 * cw] = jnp.concatenate([hi[:, cols], lo[:, cols]], axis=1)


def _sb_weights(tk, mt2, mask, ls_ref, lhs_ref, w_ref, carry_ref, h, first):
    tq = ls_ref.shape[1]
    cw = min(SB_BLK, tk)
    nc = tk // cw
    rhs = mt2 if cw == SB_BLK else jnp.concatenate([mt2[:cw], mt2[SB_BLK:SB_BLK + cw]], axis=0)
    lt = _dot(lhs_ref[h, :nc * tq, :2 * cw], rhs)
    carry = jnp.zeros((tq, SB_BLK), F32) if first else carry_ref[h]
    for c in reversed(range(nc)):
        rows = slice(c * tq, (c + 1) * tq)
        cols = slice(c * cw, (c + 1) * cw)
        w = jnp.exp(ls_ref[h, :, cols] + lt[rows, :cw] + carry[:, :cw])
        if mask is not None:
            w = jnp.where(mask[:, cols], w, 0.0)
        w_ref[h, :, cols] = w.astype(BF16)
        carry = carry + lt[rows, SB_BLK:]
    carry_ref[h] = carry


def _sb_values(tk, vb, w_ref, acc_ref, h, first):
    pv = _dot(w_ref[h, :, :tk], vb)
    acc_ref[h] = pv if first else acc_ref[h] + pv


def _sb_add_span(q_of, k_of, v_of, n_heads, tk, mt2, mask, first, ls_ref, lhs_ref, w_ref,
                 carry_ref, acc_ref):
    for h in range(n_heads):
        _sb_scores(q_of(h), k_of(h), mask, ls_ref, lhs_ref, h)
    for h in range(n_heads):
        _sb_weights(tk, mt2, mask, ls_ref, lhs_ref, w_ref, carry_ref, h, first)
    for h in range(n_heads):
        _sb_values(tk, v_of(h), w_ref, acc_ref, h, first)


def _strict_lower(n):
    row = lax.broadcasted_iota(jnp.int32, (n, n), 0)
    col = lax.broadcasted_iota(jnp.int32, (n, n), 1)
    return col < row


def _sb_scratch(n_heads, tq, tk):
    return [pltpu.VMEM((n_heads, tq, SB_BLK), F32),
            pltpu.VMEM((n_heads, tq, HEAD_DIM), F32),
            pltpu.VMEM((n_heads, tq, tk), F32),
            pltpu.VMEM((n_heads, (tk // SB_BLK) * tq, 2 * SB_BLK), BF16),
            pltpu.VMEM((n_heads, tq, tk), BF16)]


def _sb_prompt_kernel(q_ref, k_ref, v_ref, mt_ref, o_ref, carry_ref, acc_ref, ls_ref, lhs_ref, w_ref):
    i = pl.program_id(2)
    tq = q_ref.shape[0]
    n_heads = q_ref.shape[1] // HEAD_DIM
    mt2 = mt_ref[...]
    heads = [slice(h * HEAD_DIM, (h + 1) * HEAD_DIM) for h in range(n_heads)]

    def add_span(rows, mask, first):
        _sb_add_span(lambda h: q_ref[:, heads[h]], lambda h: k_ref[rows, heads[h]],
                     lambda h: v_ref[rows, heads[h]], n_heads, tq, mt2, mask, first,
                     ls_ref, lhs_ref, w_ref, carry_ref, acc_ref)

    add_span(pl.ds(pl.multiple_of(i * tq, tq), tq), _strict_lower(tq), True)

    def body(s, c):
        add_span(pl.ds(pl.multiple_of((i - 1 - s) * tq, tq), tq), None, False)
        return c

    lax.fori_loop(0, i, body, 0)
    for h in range(n_heads):
        o_ref[:, heads[h]] = acc_ref[h].astype(o_ref.dtype)


def _sb_prompt(qb, kb, vb, mt2, batch, seq):
    tq = min(SB_TQ, seq)
    nq = seq // tq
    hg = SB_HEAD_GROUP
    gw = hg * HEAD_DIM
    q_spec = pl.BlockSpec((tq, gw), lambda b, g, i: (b * nq + i, g))
    kv_spec = pl.BlockSpec((seq, gw), lambda b, g, i: (b, g))
    return pl.pallas_call(
        _sb_prompt_kernel, grid=(batch, SB_HEADS // hg, nq),
        in_specs=[q_spec, kv_spec, kv_spec, _resident((2 * SB_BLK, 2 * SB_BLK))],
        out_specs=q_spec,
        out_shape=jax.ShapeDtypeStruct((batch * seq, SB_WIDTH), BF16),
        scratch_shapes=_sb_scratch(hg, tq, tq),
        compiler_params=_cparams(3), name="sb_prompt",
    )(qb, kb, vb, mt2)


def _sb_sample_kernel(q_ref, kn_ref, vn_ref, kp_ref, vp_ref, mt_ref, o_ref,
                      carry_ref, acc_ref, ls_ref, lhs_ref, w_ref):
    c = pl.program_id(1)
    tq = q_ref.shape[0]
    n_pos = kp_ref.shape[0] // SB_HEADS
    mt2 = mt_ref[...]
    heads = [slice(h * HEAD_DIM, (h + 1) * HEAD_DIM) for h in range(SB_HEADS)]
    scratch = (ls_ref, lhs_ref, w_ref, carry_ref, acc_ref)

    @pl.when(c == 0)
    def _():
        _sb_add_span(lambda h: q_ref[:, heads[h]], lambda h: kn_ref[:, heads[h]],
                     lambda h: vn_ref[:, heads[h]], SB_HEADS, tq, mt2, _strict_lower(tq), True,
                     *scratch)

    past_rows = lambda h: pl.ds(h, n_pos, stride=SB_HEADS)
    _sb_add_span(lambda h: q_ref[:, heads[h]], lambda h: kp_ref[past_rows(h), :].astype(BF16),
                 lambda h: vp_ref[past_rows(h), :].astype(BF16), SB_HEADS, n_pos, mt2, None, False,
                 *scratch)

    @pl.when(c == pl.num_programs(1) - 1)
    def _():
        for h in range(SB_HEADS):
            o_ref[:, heads[h]] = acc_ref[h].astype(o_ref.dtype)


def _sb_sample(qb, kb, vb, past_k, past_v, layer, mt2, batch, tq, *, chunk):
    past = past_k.shape[2] // SB_HEADS
    n_chunks = past // chunk
    new = pl.BlockSpec((tq, SB_WIDTH), lambda b, c: (b, 0))
    cache = pl.BlockSpec((None, None, chunk * SB_HEADS, HEAD_DIM),
                         lambda b, c: (layer, b, n_chunks - 1 - c, 0))
    return pl.pallas_call(
        _sb_sample_kernel, grid=(batch, n_chunks),
        in_specs=[new, new, new, cache, cache, _resident((2 * SB_BLK, 2 * SB_BLK))],
        out_specs=new,
        out_shape=jax.ShapeDtypeStruct((batch * tq, SB_WIDTH), BF16),
        scratch_shapes=_sb_scratch(SB_HEADS, tq, chunk),
        compiler_params=_cparams(2), name="sb_sample",
    )(qb, kb, vb, past_k, past_v, mt2)


def _rec_heads(q, k, v, log_g, gate, norm_g, lbd, st_ref, o_ref):
    tc = q.shape[0]
    hi, lo = _split_bf16(log_g)
    cum = _dot(lbd, hi) + _dot(lbd, lo)
    row = lax.broadcasted_iota(jnp.int32, (REC_BLK, REC_BLK), 0)
    col = lax.broadcasted_iota(jnp.int32, (REC_BLK, REC_BLK), 1)
    causal = col <= row
    half = REC_BLK // 2
    for h in range(REC_HEADS):
        cs = slice(h * HEAD_DIM, (h + 1) * HEAD_DIM)
        for n in range(tc // REC_BLK):
            rs = slice(n * REC_BLK, (n + 1) * REC_BLK)
            c = cum[rs, cs]
            mid = c[half - 1:half, :]
            last = c[REC_BLK - 1:REC_BLK, :]
            q1 = q[rs, cs] * jnp.exp(c - mid)
            k1 = k[rs, cs] * jnp.exp(mid - c)
            vb = v[rs, cs].astype(BF16)
            s = jnp.where(causal, _dot_nt(q1.astype(BF16), k1.astype(BF16)), 0.0)
            o = _dot(s.astype(BF16), vb)
            st = st_ref[h]
            qd = q1 * jnp.exp(mid)
            o = o + _dot_nt(qd.astype(BF16), st.astype(BF16))
            ke = k1 * jnp.exp(last - mid)
            st_ref[h] = st * jnp.exp(last) + _dot_tn(vb, ke.astype(BF16))
            ms = jnp.mean(o * o, axis=-1, keepdims=True)
            o = o * lax.rsqrt(ms + LN_EPS) * norm_g
            g = gate[rs, cs]
            o_ref[rs, cs] = (o * (g * _sigmoid(g))).astype(o_ref.dtype)


def _rec_state_io(s0_ref, sout_ref, st_ref, body):
    c = pl.program_id(1)

    @pl.when(c == 0)
    def _():
        for h in range(REC_HEADS):
            st_ref[h] = s0_ref[0, h].T

    body()

    @pl.when(c == pl.num_programs(1) - 1)
    def _():
        for h in range(REC_HEADS):
            sout_ref[0, h] = st_ref[h].T


def _gla_kernel(q_ref, k_ref, v_ref, g_ref, a_ref, wa_ref, ba_ref, ng_ref, lbd_ref, s0_ref,
                o_ref, sout_ref, st_ref):
    def body():
        pre = _dot(a_ref[...].astype(BF16), wa_ref[...]) + ba_ref[...]
        log_g = _log_sigmoid(pre) * (1.0 / GLA_GATE_NORM)
        _rec_heads(q_ref[...] * ATT_SCALE, k_ref[...], v_ref[...], log_g, g_ref[...],
                   ng_ref[...], lbd_ref[...], st_ref, o_ref)
    _rec_state_io(s0_ref, sout_ref, st_ref, body)


def _hgrn_kernel(f_ref, q_ref, i_ref, g_ref, lb_ref, ng_ref, lbd_ref, s0_ref,
                 o_ref, sout_ref, st_ref):
    def body():
        lb = lb_ref[...]
        zf = f_ref[...]
        a = jnp.log(lb)
        b = jnp.log(1.0 - lb) + _log_sigmoid(zf)
        log_f = jnp.maximum(a, b) + _log1p_exp(-jnp.abs(a - b))
        k = (1.0 - lb) * _sigmoid(-zf)
        hq = q_ref[...]
        q = hq * _sigmoid(hq) * ATT_SCALE
        _rec_heads(q, k, i_ref[...], log_f, g_ref[...], ng_ref[...], lbd_ref[...], st_ref, o_ref)
    _rec_state_io(s0_ref, sout_ref, st_ref, body)


def _rec_specs(batch, t_b, tc, s_layer):
    nc = t_b // tc
    col = lambda c0: pl.BlockSpec((tc, REC_WIDTH), lambda b, c, c0=c0: (b * nc + c, c0 // REC_WIDTH))
    full = lambda shape: pl.BlockSpec(shape, lambda b, c: (0,) * len(shape))
    state = pl.BlockSpec((1, REC_HEADS, HEAD_DIM, HEAD_DIM), lambda b, c: (b, 0, 0, 0))
    state_in = pl.BlockSpec((None, 1, REC_HEADS, HEAD_DIM, HEAD_DIM),
                            lambda b, c: (s_layer, b, 0, 0, 0))
    out_specs = [pl.BlockSpec((tc, REC_WIDTH), lambda b, c: (b * nc + c, 0)), state]
    out_shape = [jax.ShapeDtypeStruct((batch * t_b, REC_WIDTH), BF16),
                 jax.ShapeDtypeStruct((batch, REC_HEADS, HEAD_DIM, HEAD_DIM), F32)]
    scratch = [pltpu.VMEM((REC_HEADS, HEAD_DIM, HEAD_DIM), F32)]
    return nc, col, full, state_in, out_specs, out_shape, scratch


def _gla(rec, wa, ba, ng, lbd, s0, s_layer, batch, t_b, tc):
    nc, col, full, state, out_specs, out_shape, scratch = _rec_specs(batch, t_b, tc, s_layer)
    a_spec = pl.BlockSpec((tc, HEAD_DIM), lambda b, c: (b * nc + c, R_GA // HEAD_DIM))
    return pl.pallas_call(
        _gla_kernel, grid=(batch, nc),
        in_specs=[col(R_GQ), col(R_GK), col(R_GV), col(R_GG), a_spec,
                  full((HEAD_DIM, REC_WIDTH)), full((1, REC_WIDTH)), full((1, HEAD_DIM)),
                  full((tc, tc)), state],
        out_specs=out_specs, out_shape=out_shape, scratch_shapes=scratch,
        compiler_params=_cparams(2), name="gla",
    )(rec, rec, rec, rec, rec, wa, ba, ng, lbd, s0)


def _hgrn(rec, lb, ng, lbd, s0, s_layer, batch, t_b, tc):
    nc, col, full, state, out_specs, out_shape, scratch = _rec_specs(batch, t_b, tc, s_layer)
    return pl.pallas_call(
        _hgrn_kernel, grid=(batch, nc),
        in_specs=[col(R_HF), col(R_HQ), col(R_HI), col(R_HO),
                  full((1, REC_WIDTH)), full((1, HEAD_DIM)), full((tc, tc)), state],
        out_specs=out_specs, out_shape=out_shape, scratch_shapes=scratch,
        compiler_params=_cparams(2), name="hgrn",
    )(rec, rec, rec, rec, lb, ng, lbd, s0)


def _mix_kernel(x_ref, ga_ref, gb_ref, gc_ref, oa_ref, ob_ref, oc_ref,
                wsb_ref, wgla_ref, whg_ref, wo_ref, g_ref, b_ref, o_ref):
    m = _sigmoid(ga_ref[...]) * _dot(oa_ref[...], wsb_ref[...])
    m = m + _sigmoid(gb_ref[...]) * _dot(ob_ref[...], wgla_ref[...])
    m = m + _sigmoid(gc_ref[...]) * _dot(oc_ref[...], whg_ref[...])
    y = ALPHA * x_ref[...] + _dot(m.astype(BF16), wo_ref[...])
    o_ref[...] = _ln_rows(y, g_ref[...], b_ref[...])


def _mix(x, gates, oa, ob, oc, wsb, wgla, whg, wo, g, b, *, tm):
    t, d = x.shape
    row = lambda w, j=0: pl.BlockSpec((tm, w), lambda i, j=j: (i, j))
    return pl.pallas_call(
        _mix_kernel, grid=(t // tm,),
        in_specs=[row(d), row(d, 0), row(d, 1), row(d, 2),
                  row(SB_WIDTH), row(REC_WIDTH), row(REC_WIDTH),
                  _resident((SB_WIDTH, d)), _resident((REC_WIDTH, d)), _resident((REC_WIDTH, d)),
                  _resident((d, d)), _resident((1, d)), _resident((1, d))],
        out_specs=row(d),
        out_shape=jax.ShapeDtypeStruct((t, d), F32),
        compiler_params=_cparams(1), name="mix",
    )(x, gates, gates, gates, oa, ob, oc, wsb, wgla, whg, wo, g.reshape(1, d), b.reshape(1, d))


def _xattn_kernel(x_ref, mk_ref, mv_ref, wq_ref, wo_ref, g_ref, b_ref, o_ref):
    x = x_ref[...]
    n_b = mk_ref.shape[0]
    rows_b = x.shape[0] // n_b
    q = _dot(x.astype(BF16), wq_ref[...]).astype(BF16)
    per_batch = []
    for r in range(n_b):
        rs = slice(r * rows_b, (r + 1) * rows_b)
        outs = []
        for h in range(XA_HEADS):
            cs = slice(h * HEAD_DIM, (h + 1) * HEAD_DIM)
            s = _dot_nt(q[rs, cs], mk_ref[r, :, cs].astype(BF16)) * ATT_SCALE
            s = s - jnp.max(s, axis=-1, keepdims=True)
            e = jnp.exp(s)
            p = e / jnp.sum(e, axis=-1, keepdims=True)
            outs.append(_dot(p.astype(BF16), mv_ref[r, :, cs].astype(BF16)))
        per_batch.append(jnp.concatenate(outs, axis=-1).astype(BF16))
    o = per_batch[0] if n_b == 1 else jnp.concatenate(per_batch, axis=0)
    y = ALPHA * x + _dot(o, wo_ref[...])
    o_ref[...] = _ln_rows(y, g_ref[...], b_ref[...])


def _xattn(x, mem_k, mem_v, wq, wo, g, b, *, batch, tm):
    t, d = x.shape
    t_b = t // batch
    n_mem = mem_k.shape[1]
    if tm <= t_b:
        n_b, nt = 1, t_b // tm
        grid = (batch, nt)
        row = pl.BlockSpec((tm, d), lambda bi, i: (bi * nt + i, 0))
    else:
        n_b = tm // t_b
        grid = (batch // n_b, 1)
        row = pl.BlockSpec((tm, d), lambda bi, i: (bi, 0))
    mem = pl.BlockSpec((n_b, n_mem, XA_WIDTH), lambda bi, i: (bi, 0, 0))
    return pl.pallas_call(
        _xattn_kernel, grid=grid,
        in_specs=[row, mem, mem, _resident((d, XA_WIDTH)), _resident((XA_WIDTH, d)),
                  _resident((1, d)), _resident((1, d))],
        out_specs=row,
        out_shape=jax.ShapeDtypeStruct((t, d), F32),
        compiler_params=_cparams(2), name="xattn",
    )(x, mem_k, mem_v, wq, wo, g.reshape(1, d), b.reshape(1, d))


def _ffn_kernel(x_ref, wg_ref, wu_ref, wd_ref, g_ref, b_ref, o_ref, ob_ref, xs_ref, acc_ref):
    f = pl.program_id(1)

    @pl.when(f == 0)
    def _():
        xs_ref[...] = x_ref[...].astype(BF16)
        acc_ref[...] = jnp.zeros_like(acc_ref)

    xs = xs_ref[...]
    gate = _dot(xs, wg_ref[...])
    up = _dot(xs, wu_ref[...])
    hidden = gate * _sigmoid(gate) * up
    acc_ref[...] += _dot(hidden.astype(BF16), wd_ref[...])

    @pl.when(f == pl.num_programs(1) - 1)
    def _():
        y = _ln_rows(ALPHA * x_ref[...] + acc_ref[...], g_ref[...], b_ref[...])
        o_ref[...] = y
        ob_ref[...] = y.astype(BF16)


def _ffn(x, w_in, w_out, g, b, *, tm, tf):
    t, d = x.shape
    nf = D_FF // tf
    row = pl.BlockSpec((tm, d), lambda i, f: (i, 0))
    vec = pl.BlockSpec((1, d), lambda i, f: (0, 0))
    return pl.pallas_call(
        _ffn_kernel, grid=(t // tm, nf),
        in_specs=[row,
                  pl.BlockSpec((d, tf), lambda i, f: (0, f)),
                  pl.BlockSpec((d, tf), lambda i, f: (0, nf + f)),
                  pl.BlockSpec((tf, d), lambda i, f: (f, 0)),
                  vec, vec],
        out_specs=[row, row],
        out_shape=[jax.ShapeDtypeStruct((t, d), F32), jax.ShapeDtypeStruct((t, d), BF16)],
        scratch_shapes=[pltpu.VMEM((tm, d), BF16), pltpu.VMEM((tm, d), F32)],
        compiler_params=_cparams(2), name="ffn",
    )(x, w_in, w_in, w_out, g.reshape(1, d), b.reshape(1, d))


def _tri_tables():
    j = lax.broadcasted_iota(jnp.int32, (SB_BLK, SB_BLK), 0)
    s = lax.broadcasted_iota(jnp.int32, (SB_BLK, SB_BLK), 1)
    mt = jnp.concatenate([(j > s).astype(BF16), jnp.ones((SB_BLK, SB_BLK), BF16)], axis=1)
    return jnp.concatenate([mt, mt], axis=0)


def _block_cumsum_table(tc):
    r = lax.broadcasted_iota(jnp.int32, (tc, tc), 0)
    c = lax.broadcasted_iota(jnp.int32, (tc, tc), 1)
    return ((c <= r) & (r // REC_BLK == c // REC_BLK)).astype(BF16)


def _layer(x, xb, lw, layer, mem_k, mem_v, s_gla, s_hg, s_layer, past, ln_in, k_all, v_all, depth,
           *, batch, t_b, cfg):
    tm = cfg["tm_proj"]
    if ln_in is not None:
        x, xb = _ln_in(x, ln_in[0], ln_in[1], tm=cfg["tm_ln"])
    gates = _mm(xb, lw["w_gates"], F32, tm=tm, tn=1024)
    qb = _mm(xb, lw["w_sq"], BF16, tm=tm, tn=SB_WIDTH)
    rec = _mm(xb, lw["w_rec"], F32, tm=tm, tn=N_REC // 3)
    k_all, v_all, kb, vb = _proj_kv(xb, lw["w_skv"], k_all, v_all, layer, depth, tm=cfg["tm_kv"])
    if past is None:
        oa = _sb_prompt(qb, kb, vb, lw["mt2"], batch, t_b)
    else:
        oa = _sb_sample(qb, kb, vb, past[0], past[1], layer, lw["mt2"], batch, t_b,
                        chunk=cfg["sb_chunk"])
    tc = cfg["tc"]
    lbd = _block_cumsum_table(tc)
    ob, s_gla_new = _gla(rec, lw["wa"], lw["ba"], lw["gla_ng"], lbd, s_gla, s_layer, batch, t_b, tc)
    oc, s_hg_new = _hgrn(rec, lw["lb"], lw["hg_ng"], lbd, s_hg, s_layer, batch, t_b, tc)
    x = _mix(x, gates, oa, ob, oc, lw["w_up_sb"], lw["w_up_gla"], lw["w_up_hg"], lw["w_o"],
             lw["ln1_g"], lw["ln1_b"], tm=cfg["tm_mix"])
    x = _xattn(x, mem_k, mem_v, lw["w_xq"], lw["w_xo"], lw["ln2_g"], lw["ln2_b"],
               batch=batch, tm=cfg["tm_xa"])
    x, xb = _ffn(x, lw["w_ffn_in"], lw["w_ffn_out"], lw["ln3_g"], lw["ln3_b"],
                 tm=cfg["tm_ffn"], tf=cfg["tf"])
    return x, xb, k_all, v_all, s_gla_new, s_hg_new


def _tile(n, pref):
    t = min(n, pref)
    while n % t:
        t //= 2
    return t


def kernel(x_prompt, x_sample, mem_prompt, cache_sb_k, cache_sb_v, cache_mem_k, cache_mem_v,
           state_gla, state_hgrn, ln_in_g, ln_in_b, w_in, gla_w_a, gla_b_a, gla_norm_g, hg_lower,
           hg_norm_g, w_up_sb, w_up_gla, w_up_hg, w_o, ln1_g, ln1_b, w_xq, w_xk, w_xv, w_xo,
           ln2_g, ln2_b, w_ffn_in, w_ffn_out, ln3_g, ln3_b):
    bp, seq, d = x_prompt.shape
    bs, tq, _ = x_sample.shape
    n_mem = mem_prompt.shape[1]
    past_len = cache_sb_k.shape[2]
    depth = w_in.shape[0]

    lb_all = jnp.cumsum(jax.nn.softmax(hg_lower.astype(F32), axis=0), axis=0)
    lb_all = lb_all - lb_all[0:1]
    mt2 = _tri_tables()
    wa_pad = jnp.zeros((depth, HEAD_DIM, REC_WIDTH), BF16).at[:, :GLA_RANK].set(gla_w_a.astype(BF16))
    ga_zero = jnp.zeros((d, GA_PAD - GLA_RANK), BF16)

    xp = x_prompt.reshape(bp * seq, d)
    xs = x_sample.reshape(bs * tq, d)
    xpb = xsb = None
    mem2d = mem_prompt.reshape(bp * n_mem, d)
    zero_state = jnp.zeros((1, bp, REC_HEADS, HEAD_DIM, HEAD_DIM), F32)
    kp_all = vp_all = ks_all = vs_all = None
    past = (cache_sb_k.reshape(depth, bs, past_len * SB_HEADS, HEAD_DIM),
            cache_sb_v.reshape(depth, bs, past_len * SB_HEADS, HEAD_DIM))

    cfg_p = dict(tm_proj=_tile(bp * seq, 1024), tm_kv=_tile(bp * seq, 512), tc=_tile(seq, 256),
                 tm_mix=_tile(bp * seq, 256), tm_xa=_tile(seq, 256),
                 tm_ffn=_tile(bp * seq, 512), tf=512, tm_ln=_tile(bp * seq, 256))
    cfg_s = dict(tm_proj=_tile(bs * tq, 1024), tm_kv=_tile(bs * tq, 512), tc=_tile(tq, 256),
                 tm_mix=_tile(bs * tq, 256), tm_xa=_tile(bs * tq, 512),
                 tm_ffn=_tile(bs * tq, 512), tf=512, tm_ln=_tile(bs * tq, 256),
                 sb_chunk=_tile(past_len, 1024))

    outs_p, outs_s = [], []
    for l in range(depth):
        w = w_in[l]
        lw = dict(
            w_gates=w[:, W_GATES:].astype(BF16), w_sq=w[:, W_SQ:W_SKV].astype(BF16),
            w_skv=w[:, W_SKV:W_REC_A].astype(BF16),
            w_rec=jnp.concatenate([w[:, W_REC_A:W_GA].astype(BF16), w[:, W_REC_H:W_GATES].astype(BF16),
                                   w[:, W_GA:W_REC_H].astype(BF16), ga_zero], axis=1),
            mt2=mt2, wa=wa_pad[l], ba=gla_b_a[l].reshape(1, REC_WIDTH),
            gla_ng=gla_norm_g[l].reshape(1, HEAD_DIM), lb=lb_all[l].reshape(1, REC_WIDTH),
            hg_ng=hg_norm_g[l].reshape(1, HEAD_DIM),
            w_up_sb=w_up_sb[l].astype(BF16), w_up_gla=w_up_gla[l].astype(BF16),
            w_up_hg=w_up_hg[l].astype(BF16), w_o=w_o[l].astype(BF16),
            ln1_g=ln1_g[l], ln1_b=ln1_b[l], w_xq=w_xq[l].astype(BF16), w_xo=w_xo[l].astype(BF16),
            ln2_g=ln2_g[l], ln2_b=ln2_b[l], w_ffn_in=w_ffn_in[l].astype(BF16),
            w_ffn_out=w_ffn_out[l].astype(BF16), ln3_g=ln3_g[l], ln3_b=ln3_b[l])
        ln_in = (ln_in_g, ln_in_b) if l == 0 else None
        tm_mem = _tile(bp * n_mem, 512)
        mem_k = _mm(mem2d, w_xk[l].astype(BF16), F32, tm=tm_mem, tn=XA_WIDTH)
        mem_v = _mm(mem2d, w_xv[l].astype(BF16), F32, tm=tm_mem, tn=XA_WIDTH)
        xp, xpb, kp_all, vp_all, gp, hp = _layer(
            xp, xpb, lw, l, mem_k.reshape(bp, n_mem, XA_WIDTH), mem_v.reshape(bp, n_mem, XA_WIDTH),
            zero_state, zero_state, 0, None, ln_in, kp_all, vp_all, depth,
            batch=bp, t_b=seq, cfg=cfg_p)
        xs, xsb, ks_all, vs_all, gs, hs = _layer(
            xs, xsb, lw, l, cache_mem_k[l].reshape(bs, n_mem, XA_WIDTH),
            cache_mem_v[l].reshape(bs, n_mem, XA_WIDTH),
            state_gla, state_hgrn, l, past, ln_in, ks_all, vs_all, depth,
            batch=bs, t_b=tq, cfg=cfg_s)
        outs_p.append((mem_k.reshape(bp, n_mem, XA_HEADS, HEAD_DIM),
                       mem_v.reshape(bp, n_mem, XA_HEADS, HEAD_DIM), gp, hp))
        outs_s.append((gs, hs))
    stk = lambda outs, i: jnp.stack([o[i] for o in outs], axis=0)
    kv_p = (depth, bp, seq, SB_HEADS, HEAD_DIM)
    kv_s = (depth, bs, tq, SB_HEADS, HEAD_DIM)
    return (xp.reshape(bp, seq, d), xs.reshape(bs, tq, d),
            kp_all.reshape(kv_p), vp_all.reshape(kv_p), stk(outs_p, 0), stk(outs_p, 1),
            stk(outs_p, 2), stk(outs_p, 3), ks_all.reshape(kv_s), vs_all.reshape(kv_s),
            stk(outs_s, 0), stk(outs_s, 1))
```

```python
import jax
import jax.numpy as jnp
from jax import lax
from jax.experimental import pallas as pl
from jax.experimental.pallas import tpu as pltpu

F32 = jnp.float32
BF16 = jnp.bfloat16

D_MODEL = 2048
DEPTH = 2
HEAD_DIM = 128
SB_HEADS = 8
SB_WIDTH = SB_HEADS * HEAD_DIM
REC_HEADS = 4
REC_WIDTH = REC_HEADS * HEAD_DIM
GLA_RANK = 16
GLA_GATE_NORM = 16.0
XA_HEADS = 4
XA_WIDTH = XA_HEADS * HEAD_DIM
D_FF = 5632
LN_EPS = 1e-5
ALPHA = (2 * DEPTH) ** 0.25
ATT_SCALE = HEAD_DIM ** -0.5

W_SQ = 0
W_SKV = SB_WIDTH
W_REC_A = 3 * SB_WIDTH
W_GA = W_REC_A + 4 * REC_WIDTH
W_REC_H = W_GA + GLA_RANK
W_GATES = W_REC_H + 4 * REC_WIDTH

GA_PAD = HEAD_DIM
R_GQ, R_GK, R_GV, R_GG, R_HF, R_HQ, R_HI, R_HO, R_GA = (i * REC_WIDTH for i in range(9))
N_REC = R_GA + GA_PAD

SB_BLK = 128
SB_TQ = 256
SB_HEAD_GROUP = 4
REC_BLK = 64
VMEM_LIMIT = 56 * 1024 * 1024


def _cparams(n_axes):
    return pltpu.CompilerParams(dimension_semantics=("arbitrary",) * n_axes,
                                vmem_limit_bytes=VMEM_LIMIT)


def _ln_rows(x, g, b):
    mu = jnp.mean(x, axis=-1, keepdims=True)
    xc = x - mu
    var = jnp.mean(xc * xc, axis=-1, keepdims=True)
    return xc * lax.rsqrt(var + LN_EPS) * g + b


def _log1p_exp(neg):
    return jnp.log(1.0 + jnp.exp(neg))


def _log_sigmoid(z):
    return jnp.minimum(z, 0.0) - _log1p_exp(-jnp.abs(z))


def _sigmoid(z):
    return 1.0 / (1.0 + jnp.exp(-z))


def _dot(a, b):
    return jnp.dot(a, b, preferred_element_type=F32)


def _dot_nt(a, b):
    return lax.dot_general(a, b, (((1,), (1,)), ((), ())), preferred_element_type=F32)


def _dot_tn(a, b):
    return lax.dot_general(a, b, (((0,), (0,)), ((), ())), preferred_element_type=F32)


def _split_bf16(x):
    hi = x.astype(BF16)
    lo = (x - hi.astype(F32)).astype(BF16)
    return hi, lo


def _resident(shape):
    return pl.BlockSpec(shape, lambda *_: (0,) * len(shape), pipeline_mode=pl.Buffered(1))


def _ln_in_kernel(x_ref, g_ref, b_ref, xn_ref, xb_ref):
    x = _ln_rows(x_ref[...], g_ref[...], b_ref[...])
    xn_ref[...] = x
    xb_ref[...] = x.astype(BF16)


def _ln_in(x, g, b, *, tm):
    t, d = x.shape
    x_spec = pl.BlockSpec((tm, d), lambda i: (i, 0))
    vec_spec = pl.BlockSpec((1, d), lambda i: (0, 0))
    return pl.pallas_call(
        _ln_in_kernel, grid=(t // tm,),
        in_specs=[x_spec, vec_spec, vec_spec], out_specs=[x_spec, x_spec],
        out_shape=[jax.ShapeDtypeStruct((t, d), F32), jax.ShapeDtypeStruct((t, d), BF16)],
        compiler_params=_cparams(1), name="ln_in",
    )(x, g.reshape(1, d), b.reshape(1, d))


def _mm_kernel(x_ref, w_ref, o_ref):
    o_ref[...] = _dot(x_ref[...].astype(BF16), w_ref[...]).astype(o_ref.dtype)


def _mm(x, w, out_dtype, *, tm, tn):
    t, d = x.shape
    n = w.shape[1]
    return pl.pallas_call(
        _mm_kernel, grid=(t // tm, n // tn),
        in_specs=[pl.BlockSpec((tm, d), lambda i, j: (i, 0)),
                  pl.BlockSpec((d, tn), lambda i, j: (0, j))],
        out_specs=pl.BlockSpec((tm, tn), lambda i, j: (i, j)),
        out_shape=jax.ShapeDtypeStruct((t, n), out_dtype),
        compiler_params=_cparams(2), name="mm",
    )(x, w)


def _proj_kv_kernel(x_ref, w_ref, *refs):
    k_nat_ref, v_nat_ref, kb_ref, vb_ref = refs[-4:]
    tm = x_ref.shape[0]
    res = _dot(x_ref[...], w_ref[...])
    kb_ref[...] = res[:, :SB_WIDTH].astype(BF16)
    vb_ref[...] = res[:, SB_WIDTH:].astype(BF16)
    for h in range(SB_HEADS):
        rows = pl.ds(h, tm, stride=SB_HEADS)
        k_nat_ref[rows, :] = res[:, h * HEAD_DIM:(h + 1) * HEAD_DIM]
        v_nat_ref[rows, :] = res[:, SB_WIDTH + h * HEAD_DIM:SB_WIDTH + (h + 1) * HEAD_DIM]


def _proj_kv(xb, w, k_all, v_all, layer, depth, *, tm):
    t, d = xb.shape
    nb = t // tm
    nat = pl.BlockSpec((tm * SB_HEADS, HEAD_DIM), lambda i: (layer * nb + i, 0))
    cols = pl.BlockSpec((tm, SB_WIDTH), lambda i: (i, 0))
    in_specs = [pl.BlockSpec((tm, d), lambda i: (i, 0)), _resident((d, 2 * SB_WIDTH))]
    args = [xb, w]
    aliases = {}
    if k_all is not None:
        in_specs += [pl.BlockSpec(memory_space=pl.ANY)] * 2
        args += [k_all, v_all]
        aliases = {2: 0, 3: 1}
    nat_shape = jax.ShapeDtypeStruct((depth * t * SB_HEADS, HEAD_DIM), F32)
    return pl.pallas_call(
        _proj_kv_kernel, grid=(nb,),
        in_specs=in_specs,
        out_specs=[nat, nat, cols, cols],
        out_shape=[nat_shape, nat_shape,
                   jax.ShapeDtypeStruct((t, SB_WIDTH), BF16), jax.ShapeDtypeStruct((t, SB_WIDTH), BF16)],
        input_output_aliases=aliases,
        compiler_params=_cparams(1), name="proj_kv",
    )(*args)


def _sb_scores(qb, kb, mask, ls_ref, lhs_ref, h):
    tq, tk = qb.shape[0], kb.shape[0]
    cw = min(SB_BLK, tk)
    z = _dot_nt(qb, kb) * ATT_SCALE
    ls = _log_sigmoid(z)
    lk = ls - z
    if mask is not None:
        lk = jnp.where(mask, lk, 0.0)
    hi, lo = _split_bf16(lk)
    ls_ref[h, :, :tk] = ls
    for c in range(tk // cw):
        cols = slice(c * cw, (c + 1) * cw)
        lhs_ref[h, c * tq:(c + 1) * tq, :2 * cw] = jnp.concatenate([hi[:, cols], lo[:, cols]], axis=1)


def _sb_weights(tk, mt2, mask, ls_ref, lhs_ref, w_ref, carry_ref, h, first):
    tq = ls_ref.shape[1]
    cw = min(SB_BLK, tk)
    nc = tk // cw
    rhs = mt2 if cw == SB_BLK else jnp.concatenate([mt2[:cw], mt2[SB_BLK:SB_BLK + cw]], axis=0)
    lt = _dot(lhs_ref[h, :nc * tq, :2 * cw], rhs)
    carry = jnp.zeros((tq, SB_BLK), F32) if first else carry_ref[h]
    for c in reversed(range(nc)):
        rows = slice(c * tq, (c + 1) * tq)
        cols = slice(c * cw, (c + 1) * cw)
        w = jnp.exp(ls_ref[h, :, cols] + lt[rows, :cw] + carry[:, :cw])
        if mask is not None:
            w = jnp.where(mask[:, cols], w, 0.0)
        w_ref[h, :, cols] = w.astype(BF16)
        carry = carry + lt[rows, SB_BLK:]
    carry_ref[h] = carry


def _sb_values(tk, vb, w_ref, acc_ref, h, first):
    pv = _dot(w_ref[h, :, :tk], vb)
    acc_ref[h] = pv if first else acc_ref[h] + pv


def _sb_add_span(q_of, k_of, v_of, n_heads, tk, mt2, mask, first, ls_ref, lhs_ref, w_ref,
                 carry_ref, acc_ref):
    for h in range(n_heads):
        _sb_scores(q_of(h), k_of(h), mask, ls_ref, lhs_ref, h)
    for h in range(n_heads):
        _sb_weights(tk, mt2, mask, ls_ref, lhs_ref, w_ref, carry_ref, h, first)
    for h in range(n_heads):
        _sb_values(tk, v_of(h), w_ref, acc_ref, h, first)


def _strict_lower(n):
    row = lax.broadcasted_iota(jnp.int32, (n, n), 0)
    col = lax.broadcasted_iota(jnp.int32, (n, n), 1)
    return col < row


def _sb_scratch(n_heads, tq, tk):
    return [pltpu.VMEM((n_heads, tq, SB_BLK), F32),
            pltpu.VMEM((n_heads, tq, HEAD_DIM), F32),
            pltpu.VMEM((n_heads, tq, tk), F32),
            pltpu.VMEM((n_heads, (tk // SB_BLK) * tq, 2 * SB_BLK), BF16),
            pltpu.VMEM((n_heads, tq, tk), BF16)]


def _sb_prompt_kernel(q_ref, k_ref, v_ref, mt_ref, o_ref, carry_ref, acc_ref, ls_ref, lhs_ref, w_ref):
    i = pl.program_id(2)
    tq = q_ref.shape[0]
    n_heads = q_ref.shape[1] // HEAD_DIM
    mt2 = mt_ref[...]
    heads = [slice(h * HEAD_DIM, (h + 1) * HEAD_DIM) for h in range(n_heads)]

    def add_span(rows, mask, first):
        _sb_add_span(lambda h: q_ref[:, heads[h]], lambda h: k_ref[rows, heads[h]],
                     lambda h: v_ref[rows, heads[h]], n_heads, tq, mt2, mask, first,
                     ls_ref, lhs_ref, w_ref, carry_ref, acc_ref)

    add_span(pl.ds(pl.multiple_of(i * tq, tq), tq), _strict_lower(tq), True)

    def body(s, c):
        add_span(pl.ds(pl.multiple_of((i - 1 - s) * tq, tq), tq), None, False)
        return c

    lax.fori_loop(0, i, body, 0)
    for h in range(n_heads):
        o_ref[:, heads[h]] = acc_ref[h].astype(o_ref.dtype)


def _sb_prompt(qb, kb, vb, mt2, batch, seq):
    tq = min(SB_TQ, seq)
    nq = seq // tq
    hg = SB_HEAD_GROUP
    gw = hg * HEAD_DIM
    q_spec = pl.BlockSpec((tq, gw), lambda b, g, i: (b * nq + i, g))
    kv_spec = pl.BlockSpec((seq, gw), lambda b, g, i: (b, g))
    return pl.pallas_call(
        _sb_prompt_kernel, grid=(batch, SB_HEADS // hg, nq),
        in_specs=[q_spec, kv_spec, kv_spec, _resident((2 * SB_BLK, 2 * SB_BLK))],
        out_specs=q_spec,
        out_shape=jax.ShapeDtypeStruct((batch * seq, SB_WIDTH), BF16),
        scratch_shapes=_sb_scratch(hg, tq, tq),
        compiler_params=_cparams(3), name="sb_prompt",
    )(qb, kb, vb, mt2)


def _sb_sample_kernel(q_ref, kn_ref, vn_ref, kp_ref, vp_ref, mt_ref, o_ref,
                      carry_ref, acc_ref, ls_ref, lhs_ref, w_ref):
    c = pl.program_id(1)
    tq = q_ref.shape[0]
    n_pos = kp_ref.shape[0] // SB_HEADS
    mt2 = mt_ref[...]
    heads = [slice(h * HEAD_DIM, (h + 1) * HEAD_DIM) for h in range(SB_HEADS)]
    scratch = (ls_ref, lhs_ref, w_ref, carry_ref, acc_ref)

    @pl.when(c == 0)
    def _():
        _sb_add_span(lambda h: q_ref[:, heads[h]], lambda h: kn_ref[:, heads[h]],
                     lambda h: vn_ref[:, heads[h]], SB_HEADS, tq, mt2, _strict_lower(tq), True,
                     *scratch)

    past_rows = lambda h: pl.ds(h, n_pos, stride=SB_HEADS)
    _sb_add_span(lambda h: q_ref[:, heads[h]], lambda h: kp_ref[past_rows(h), :].astype(BF16),
                 lambda h: vp_ref[past_rows(h), :].astype(BF16), SB_HEADS, n_pos, mt2, None, False,
                 *scratch)

    @pl.when(c == pl.num_programs(1) - 1)
    def _():
        for h in range(SB_HEADS):
            o_ref[:, heads[h]] = acc_ref[h].astype(o_ref.dtype)


def _sb_sample(qb, kb, vb, past_k, past_v, layer, mt2, batch, tq, *, chunk):
    past = past_k.shape[2] // SB_HEADS
    n_chunks = past // chunk
    new = pl.BlockSpec((tq, SB_WIDTH), lambda b, c: (b, 0))
    cache = pl.BlockSpec((None, None, chunk * SB_HEADS, HEAD_DIM),
                         lambda b, c: (layer, b, n_chunks - 1 - c, 0))
    return pl.pallas_call(
        _sb_sample_kernel, grid=(batch, n_chunks),
        in_specs=[new, new, new, cache, cache, _resident((2 * SB_BLK, 2 * SB_BLK))],
        out_specs=new,
        out_shape=jax.ShapeDtypeStruct((batch * tq, SB_WIDTH), BF16),
        scratch_shapes=_sb_scratch(SB_HEADS, tq, chunk),
        compiler_params=_cparams(2), name="sb_sample",
    )(qb, kb, vb, past_k, past_v, mt2)


def _rec_heads(q, k, v, log_g, gate, norm_g, lbd, st_ref, o_ref):
    tc = q.shape[0]
    hi, lo = _split_bf16(log_g)
    cum = _dot(lbd, hi) + _dot(lbd, lo)
    row = lax.broadcasted_iota(jnp.int32, (REC_BLK, REC_BLK), 0)
    col = lax.broadcasted_iota(jnp.int32, (REC_BLK, REC_BLK), 1)
    causal = col <= row
    half = REC_BLK // 2
    blocks = [(n, h) for n in range(tc // REC_BLK) for h in range(REC_HEADS)]
    rows = lambda n: slice(n * REC_BLK, (n + 1) * REC_BLK)
    cols = lambda h: slice(h * HEAD_DIM, (h + 1) * HEAD_DIM)
    prep = {}
    for n, h in blocks:
        c = cum[rows(n), cols(h)]
        mid = c[half - 1:half, :]
        last = c[REC_BLK - 1:REC_BLK, :]
        q1 = q[rows(n), cols(h)] * jnp.exp(c - mid)
        k1 = k[rows(n), cols(h)] * jnp.exp(mid - c)
        vb = v[rows(n), cols(h)].astype(BF16)
        s = jnp.where(causal, _dot_nt(q1.astype(BF16), k1.astype(BF16)), 0.0)
        o_intra = _dot(s.astype(BF16), vb)
        qd = (q1 * jnp.exp(mid)).astype(BF16)
        ds = _dot_tn(vb, (k1 * jnp.exp(last - mid)).astype(BF16))
        prep[n, h] = (o_intra, qd, ds, jnp.exp(last))
    for n, h in blocks:
        o_intra, qd, ds, decay = prep[n, h]
        st = st_ref[h]
        o = o_intra + _dot_nt(qd, st.astype(BF16))
        st_ref[h] = st * decay + ds
        ms = jnp.mean(o * o, axis=-1, keepdims=True)
        o = o * lax.rsqrt(ms + LN_EPS) * norm_g
        g = gate[rows(n), cols(h)]
        o_ref[rows(n), cols(h)] = (o * (g * _sigmoid(g))).astype(o_ref.dtype)


def _rec_state_io(s0_ref, sout_ref, st_ref, body):
    c = pl.program_id(1)

    @pl.when(c == 0)
    def _():
        for h in range(REC_HEADS):
            st_ref[h] = s0_ref[0, h].T

    body()

    @pl.when(c == pl.num_programs(1) - 1)
    def _():
        for h in range(REC_HEADS):
            sout_ref[0, h] = st_ref[h].T


def _gla_kernel(q_ref, k_ref, v_ref, g_ref, a_ref, wa_ref, ba_ref, ng_ref, lbd_ref, s0_ref,
                o_ref, sout_ref, st_ref):
    def body():
        pre = _dot(a_ref[...].astype(BF16), wa_ref[...]) + ba_ref[...]
        log_g = _log_sigmoid(pre) * (1.0 / GLA_GATE_NORM)
        _rec_heads(q_ref[...] * ATT_SCALE, k_ref[...], v_ref[...], log_g, g_ref[...],
                   ng_ref[...], lbd_ref[...], st_ref, o_ref)
    _rec_state_io(s0_ref, sout_ref, st_ref, body)


def _hgrn_kernel(f_ref, q_ref, i_ref, g_ref, lb_ref, ng_ref, lbd_ref, s0_ref,
                 o_ref, sout_ref, st_ref):
    def body():
        lb = lb_ref[...]
        zf = f_ref[...]
        a = jnp.log(lb)
        b = jnp.log(1.0 - lb) + _log_sigmoid(zf)
        log_f = jnp.maximum(a, b) + _log1p_exp(-jnp.abs(a - b))
        k = (1.0 - lb) * _sigmoid(-zf)
        hq = q_ref[...]
        q = hq * _sigmoid(hq) * ATT_SCALE
        _rec_heads(q, k, i_ref[...], log_f, g_ref[...], ng_ref[...], lbd_ref[...], st_ref, o_ref)
    _rec_state_io(s0_ref, sout_ref, st_ref, body)


def _rec_specs(batch, t_b, tc, s_layer):
    nc = t_b // tc
    col = lambda c0: pl.BlockSpec((tc, REC_WIDTH), lambda b, c, c0=c0: (b * nc + c, c0 // REC_WIDTH))
    full = lambda shape: pl.BlockSpec(shape, lambda b, c: (0,) * len(shape))
    state = pl.BlockSpec((1, REC_HEADS, HEAD_DIM, HEAD_DIM), lambda b, c: (b, 0, 0, 0))
    state_in = pl.BlockSpec((None, 1, REC_HEADS, HEAD_DIM, HEAD_DIM),
                            lambda b, c: (s_layer, b, 0, 0, 0))
    out_specs = [pl.BlockSpec((tc, REC_WIDTH), lambda b, c: (b * nc + c, 0)), state]
    out_shape = [jax.ShapeDtypeStruct((batch * t_b, REC_WIDTH), BF16),
                 jax.ShapeDtypeStruct((batch, REC_HEADS, HEAD_DIM, HEAD_DIM), F32)]
    scratch = [pltpu.VMEM((REC_HEADS, HEAD_DIM, HEAD_DIM), F32)]
    return nc, col, full, state_in, out_specs, out_shape, scratch


def _gla(rec, wa, ba, ng, lbd, s0, s_layer, batch, t_b, tc):
    nc, col, full, state, out_specs, out_shape, scratch = _rec_specs(batch, t_b, tc, s_layer)
    a_spec = pl.BlockSpec((tc, HEAD_DIM), lambda b, c: (b * nc + c, R_GA // HEAD_DIM))
    return pl.pallas_call(
        _gla_kernel, grid=(batch, nc),
        in_specs=[col(R_GQ), col(R_GK), col(R_GV), col(R_GG), a_spec,
                  full((HEAD_DIM, REC_WIDTH)), full((1, REC_WIDTH)), full((1, HEAD_DIM)),
                  full((tc, tc)), state],
        out_specs=out_specs, out_shape=out_shape, scratch_shapes=scratch,
        compiler_params=_cparams(2), name="gla",
    )(rec, rec, rec, rec, rec, wa, ba, ng, lbd, s0)


def _hgrn(rec, lb, ng, lbd, s0, s_layer, batch, t_b, tc):
    nc, col, full, state, out_specs, out_shape, scratch = _rec_specs(batch, t_b, tc, s_layer)
    return pl.pallas_call(
        _hgrn_kernel, grid=(batch, nc),
        in_specs=[col(R_HF), col(R_HQ), col(R_HI), col(R_HO),
                  full((1, REC_WIDTH)), full((1, HEAD_DIM)), full((tc, tc)), state],
        out_specs=out_specs, out_shape=out_shape, scratch_shapes=scratch,
        compiler_params=_cparams(2), name="hgrn",
    )(rec, rec, rec, rec, lb, ng, lbd, s0)


def _mix_kernel(x_ref, ga_ref, gb_ref, gc_ref, oa_ref, ob_ref, oc_ref,
                wsb_ref, wgla_ref, whg_ref, wo_ref, g_ref, b_ref, o_ref):
    m = _sigmoid(ga_ref[...]) * _dot(oa_ref[...], wsb_ref[...])
    m = m + _sigmoid(gb_ref[...]) * _dot(ob_ref[...], wgla_ref[...])
    m = m + _sigmoid(gc_ref[...]) * _dot(oc_ref[...], whg_ref[...])
    y = ALPHA * x_ref[...] + _dot(m.astype(BF16), wo_ref[...])
    o_ref[...] = _ln_rows(y, g_ref[...], b_ref[...])


def _mix(x, gates, oa, ob, oc, wsb, wgla, whg, wo, g, b, *, tm):
    t, d = x.shape
    row = lambda w, j=0: pl.BlockSpec((tm, w), lambda i, j=j: (i, j))
    return pl.pallas_call(
        _mix_kernel, grid=(t // tm,),
        in_specs=[row(d), row(d, 0), row(d, 1), row(d, 2),
                  row(SB_WIDTH), row(REC_WIDTH), row(REC_WIDTH),
                  _resident((SB_WIDTH, d)), _resident((REC_WIDTH, d)), _resident((REC_WIDTH, d)),
                  _resident((d, d)), _resident((1, d)), _resident((1, d))],
        out_specs=row(d),
        out_shape=jax.ShapeDtypeStruct((t, d), F32),
        compiler_params=_cparams(1), name="mix",
    )(x, gates, gates, gates, oa, ob, oc, wsb, wgla, whg, wo, g.reshape(1, d), b.reshape(1, d))


def _xattn_kernel(x_ref, mk_ref, mv_ref, wq_ref, wo_ref, g_ref, b_ref, o_ref):
    x = x_ref[...]
    n_b = mk_ref.shape[0]
    rows_b = x.shape[0] // n_b
    q = _dot(x.astype(BF16), wq_ref[...]).astype(BF16)
    pairs = [(r, h) for r in range(n_b) for h in range(XA_HEADS)]
    rows = lambda r: slice(r * rows_b, (r + 1) * rows_b)
    cols = lambda h: slice(h * HEAD_DIM, (h + 1) * HEAD_DIM)
    scores = {(r, h): _dot_nt(q[rows(r), cols(h)], mk_ref[r, :, cols(h)].astype(BF16))
              for r, h in pairs}
    probs = {}
    for p in pairs:
        s = scores[p] * ATT_SCALE
        e = jnp.exp(s - jnp.max(s, axis=-1, keepdims=True))
        probs[p] = (e / jnp.sum(e, axis=-1, keepdims=True)).astype(BF16)
    outs = {(r, h): _dot(probs[r, h], mv_ref[r, :, cols(h)].astype(BF16)).astype(BF16)
            for r, h in pairs}
    per_batch = [jnp.concatenate([outs[r, h] for h in range(XA_HEADS)], axis=-1)
                 for r in range(n_b)]
    o = per_batch[0] if n_b == 1 else jnp.concatenate(per_batch, axis=0)
    y = ALPHA * x + _dot(o, wo_ref[...])
    o_ref[...] = _ln_rows(y, g_ref[...], b_ref[...])


def _xattn(x, mem_k, mem_v, wq, wo, g, b, *, batch, tm):
    t, d = x.shape
    t_b = t // batch
    n_mem = mem_k.shape[1]
    if tm <= t_b:
        n_b, nt = 1, t_b // tm
        grid = (batch, nt)
        row = pl.BlockSpec((tm, d), lambda bi, i: (bi * nt + i, 0))
    else:
        n_b = tm // t_b
        grid = (batch // n_b, 1)
        row = pl.BlockSpec((tm, d), lambda bi, i: (bi, 0))
    mem = pl.BlockSpec((n_b, n_mem, XA_WIDTH), lambda bi, i: (bi, 0, 0))
    return pl.pallas_call(
        _xattn_kernel, grid=grid,
        in_specs=[row, mem, mem, _resident((d, XA_WIDTH)), _resident((XA_WIDTH, d)),
                  _resident((1, d)), _resident((1, d))],
        out_specs=row,
        out_shape=jax.ShapeDtypeStruct((t, d), F32),
        compiler_params=_cparams(2), name="xattn",
    )(x, mem_k, mem_v, wq, wo, g.reshape(1, d), b.reshape(1, d))


def _ffn_kernel(x_ref, wg_ref, wu_ref, wd_ref, g_ref, b_ref, o_ref, ob_ref, xs_ref, acc_ref):
    f = pl.program_id(1)

    @pl.when(f == 0)
    def _():
        xs_ref[...] = x_ref[...].astype(BF16)
        acc_ref[...] = jnp.zeros_like(acc_ref)

    xs = xs_ref[...]
    gate = _dot(xs, wg_ref[...])
    up = _dot(xs, wu_ref[...])
    hidden = gate * _sigmoid(gate) * up
    acc_ref[...] += _dot(hidden.astype(BF16), wd_ref[...])

    @pl.when(f == pl.num_programs(1) - 1)
    def _():
        y = _ln_rows(ALPHA * x_ref[...] + acc_ref[...], g_ref[...], b_ref[...])
        o_ref[...] = y
        ob_ref[...] = y.astype(BF16)


def _ffn(x, w_in, w_out, g, b, *, tm, tf):
    t, d = x.shape
    nf = D_FF // tf
    row = pl.BlockSpec((tm, d), lambda i, f: (i, 0))
    vec = pl.BlockSpec((1, d), lambda i, f: (0, 0))
    return pl.pallas_call(
        _ffn_kernel, grid=(t // tm, nf),
        in_specs=[row,
                  pl.BlockSpec((d, tf), lambda i, f: (0, f)),
                  pl.BlockSpec((d, tf), lambda i, f: (0, nf + f)),
                  pl.BlockSpec((tf, d), lambda i, f: (f, 0)),
                  vec, vec],
        out_specs=[row, row],
        out_shape=[jax.ShapeDtypeStruct((t, d), F32), jax.ShapeDtypeStruct((t, d), BF16)],
        scratch_shapes=[pltpu.VMEM((tm, d), BF16), pltpu.VMEM((tm, d), F32)],
        compiler_params=_cparams(2), name="ffn",
    )(x, w_in, w_in, w_out, g.reshape(1, d), b.reshape(1, d))


def _tri_tables():
    j = lax.broadcasted_iota(jnp.int32, (SB_BLK, SB_BLK), 0)
    s = lax.broadcasted_iota(jnp.int32, (SB_BLK, SB_BLK), 1)
    mt = jnp.concatenate([(j > s).astype(BF16), jnp.ones((SB_BLK, SB_BLK), BF16)], axis=1)
    return jnp.concatenate([mt, mt], axis=0)


def _block_cumsum_table(tc):
    r = lax.broadcasted_iota(jnp.int32, (tc, tc), 0)
    c = lax.broadcasted_iota(jnp.int32, (tc, tc), 1)
    return ((c <= r) & (r // REC_BLK == c // REC_BLK)).astype(BF16)


def _layer(x, xb, lw, layer, mem_k, mem_v, s_gla, s_hg, s_layer, past, ln_in, k_all, v_all, depth,
           *, batch, t_b, cfg):
    tm = cfg["tm_proj"]
    if ln_in is not None:
        x, xb = _ln_in(x, ln_in[0], ln_in[1], tm=cfg["tm_ln"])
    gates = _mm(xb, lw["w_gates"], F32, tm=tm, tn=2048)
    qb = _mm(xb, lw["w_sq"], BF16, tm=tm, tn=SB_WIDTH)
    rec = _mm(xb, lw["w_rec"], F32, tm=tm, tn=N_REC // 3)
    k_all, v_all, kb, vb = _proj_kv(xb, lw["w_skv"], k_all, v_all, layer, depth, tm=cfg["tm_kv"])
    if past is None:
        oa = _sb_prompt(qb, kb, vb, lw["mt2"], batch, t_b)
    else:
        oa = _sb_sample(qb, kb, vb, past[0], past[1], layer, lw["mt2"], batch, t_b,
                        chunk=cfg["sb_chunk"])
    tc = cfg["tc"]
    lbd = _block_cumsum_table(tc)
    ob, s_gla_new = _gla(rec, lw["wa"], lw["ba"], lw["gla_ng"], lbd, s_gla, s_layer, batch, t_b, tc)
    oc, s_hg_new = _hgrn(rec, lw["lb"], lw["hg_ng"], lbd, s_hg, s_layer, batch, t_b, tc)
    x = _mix(x, gates, oa, ob, oc, lw["w_up_sb"], lw["w_up_gla"], lw["w_up_hg"], lw["w_o"],
             lw["ln1_g"], lw["ln1_b"], tm=cfg["tm_mix"])
    x = _xattn(x, mem_k, mem_v, lw["w_xq"], lw["w_xo"], lw["ln2_g"], lw["ln2_b"],
               batch=batch, tm=cfg["tm_xa"])
    x, xb = _ffn(x, lw["w_ffn_in"], lw["w_ffn_out"], lw["ln3_g"], lw["ln3_b"],
                 tm=cfg["tm_ffn"], tf=cfg["tf"])
    return x, xb, k_all, v_all, s_gla_new, s_hg_new


def _tile(n, pref):
    t = min(n, pref)
    while n % t:
        t //= 2
    return t


def kernel(x_prompt, x_sample, mem_prompt, cache_sb_k, cache_sb_v, cache_mem_k, cache_mem_v,
           state_gla, state_hgrn, ln_in_g, ln_in_b, w_in, gla_w_a, gla_b_a, gla_norm_g, hg_lower,
           hg_norm_g, w_up_sb, w_up_gla, w_up_hg, w_o, ln1_g, ln1_b, w_xq, w_xk, w_xv, w_xo,
           ln2_g, ln2_b, w_ffn_in, w_ffn_out, ln3_g, ln3_b):
    bp, seq, d = x_prompt.shape
    bs, tq, _ = x_sample.shape
    n_mem = mem_prompt.shape[1]
    past_len = cache_sb_k.shape[2]
    depth = w_in.shape[0]

    lb_all = jnp.cumsum(jax.nn.softmax(hg_lower.astype(F32), axis=0), axis=0)
    lb_all = lb_all - lb_all[0:1]
    mt2 = _tri_tables()
    wa_pad = jnp.zeros((depth, HEAD_DIM, REC_WIDTH), BF16).at[:, :GLA_RANK].set(gla_w_a.astype(BF16))
    ga_zero = jnp.zeros((d, GA_PAD - GLA_RANK), BF16)

    xp = x_prompt.reshape(bp * seq, d)
    xs = x_sample.reshape(bs * tq, d)
    xpb = xsb = None
    mem2d = mem_prompt.reshape(bp * n_mem, d)
    zero_state = jnp.zeros((1, bp, REC_HEADS, HEAD_DIM, HEAD_DIM), F32)
    kp_all = vp_all = ks_all = vs_all = None
    past = (cache_sb_k.reshape(depth, bs, past_len * SB_HEADS, HEAD_DIM),
            cache_sb_v.reshape(depth, bs, past_len * SB_HEADS, HEAD_DIM))

    cfg_p = dict(tm_proj=_tile(bp * seq, 1024), tm_kv=_tile(bp * seq, 512), tc=_tile(seq, 256),
                 tm_mix=_tile(bp * seq, 256), tm_xa=_tile(seq, 256),
                 tm_ffn=_tile(bp * seq, 512), tf=512, tm_ln=_tile(bp * seq, 256))
    cfg_s = dict(tm_proj=_tile(bs * tq, 1024), tm_kv=_tile(bs * tq, 512), tc=_tile(tq, 256),
                 tm_mix=_tile(bs * tq, 256), tm_xa=_tile(bs * tq, 512),
                 tm_ffn=_tile(bs * tq, 512), tf=512, tm_ln=_tile(bs * tq, 256),
                 sb_chunk=_tile(past_len, 1024))

    outs_p, outs_s = [], []
    for l in range(depth):
        w = w_in[l]
        lw = dict(
            w_gates=w[:, W_GATES:].astype(BF16), w_sq=w[:, W_SQ:W_SKV].astype(BF16),
            w_skv=w[:, W_SKV:W_REC_A].astype(BF16),
            w_rec=jnp.concatenate([w[:, W_REC_A:W_GA].astype(BF16), w[:, W_REC_H:W_GATES].astype(BF16),
                                   w[:, W_GA:W_REC_H].astype(BF16), ga_zero], axis=1),
            mt2=mt2, wa=wa_pad[l], ba=gla_b_a[l].reshape(1, REC_WIDTH),
            gla_ng=gla_norm_g[l].reshape(1, HEAD_DIM), lb=lb_all[l].reshape(1, REC_WIDTH),
            hg_ng=hg_norm_g[l].reshape(1, HEAD_DIM),
            w_up_sb=w_up_sb[l].astype(BF16), w_up_gla=w_up_gla[l].astype(BF16),
            w_up_hg=w_up_hg[l].astype(BF16), w_o=w_o[l].astype(BF16),
            ln1_g=ln1_g[l], ln1_b=ln1_b[l], w_xq=w_xq[l].astype(BF16), w_xo=w_xo[l].astype(BF16),
            ln2_g=ln2_g[l], ln2_b=ln2_b[l], w_ffn_in=w_ffn_in[l].astype(BF16),
            w_ffn_out=w_ffn_out[l].astype(BF16), ln3_g=ln3_g[l], ln3_b=ln3_b[l])
        ln_in = (ln_in_g, ln_in_b) if l == 0 else None
        tm_mem = _tile(bp * n_mem, 512)
        mem_k = _mm(mem2d, w_xk[l].astype(BF16), F32, tm=tm_mem, tn=XA_WIDTH)
        mem_v = _mm(mem2d, w_xv[l].astype(BF16), F32, tm=tm_mem, tn=XA_WIDTH)
        xp, xpb, kp_all, vp_all, gp, hp = _layer(
            xp, xpb, lw, l, mem_k.reshape(bp, n_mem, XA_WIDTH), mem_v.reshape(bp, n_mem, XA_WIDTH),
            zero_state, zero_state, 0, None, ln_in, kp_all, vp_all, depth,
            batch=bp, t_b=seq, cfg=cfg_p)
        xs, xsb, ks_all, vs_all, gs, hs = _layer(
            xs, xsb, lw, l, cache_mem_k[l].reshape(bs, n_mem, XA_WIDTH),
            cache_mem_v[l].reshape(bs, n_mem, XA_WIDTH),
            state_gla, state_hgrn, l, past, ln_in, ks_all, vs_all, depth,
            batch=bs, t_b=tq, cfg=cfg_s)
        outs_p.append((mem_k.reshape(bp, n_mem, XA_HEADS, HEAD_DIM),
                       mem_v.reshape(bp, n_mem, XA_HEADS, HEAD_DIM), gp, hp))
        outs_s.append((gs, hs))
    stk = lambda outs, i: jnp.stack([o[i] for o in outs], axis=0)
    kv_p = (depth, bp, seq, SB_HEADS, HEAD_DIM)
    kv_s = (depth, bs, tq, SB_HEADS, HEAD_DIM)
    return (xp.reshape(bp, seq, d), xs.reshape(bs, tq, d),
            kp_all.reshape(kv_p), vp_all.reshape(kv_p), stk(outs_p, 0), stk(outs_p, 1),
            stk(outs_p, 2), stk(outs_p, 3), ks_all.reshape(kv_s), vs_all.reshape(kv_s),
            stk(outs_s, 0), stk(outs_s, 1))
```

```python
import functools

import jax
import jax.numpy as jnp
from jax import lax
from jax.experimental import pallas as pl
from jax.experimental.pallas import tpu as pltpu

F32 = jnp.float32
BF16 = jnp.bfloat16

D_MODEL = 2048
DEPTH = 2
HEAD_DIM = 128
SB_HEADS = 8
SB_WIDTH = SB_HEADS * HEAD_DIM
REC_HEADS = 4
REC_WIDTH = REC_HEADS * HEAD_DIM
GLA_RANK = 16
GLA_GATE_NORM = 16.0
XA_HEADS = 4
XA_WIDTH = XA_HEADS * HEAD_DIM
D_FF = 5632
LN_EPS = 1e-5
ALPHA = (2 * DEPTH) ** 0.25
ATT_SCALE = HEAD_DIM ** -0.5

W_SQ = 0
W_REC_A = 3 * SB_WIDTH
W_GA = W_REC_A + 4 * REC_WIDTH
W_REC_H = W_GA + GLA_RANK
W_GATES = W_REC_H + 4 * REC_WIDTH

GA_PAD = HEAD_DIM
R_GQ, R_GK, R_GV, R_GG, R_HF, R_HQ, R_HI, R_HO, R_GA = (i * REC_WIDTH for i in range(9))
N_REC = R_GA + GA_PAD

SB_BLK = 128
SB_TQ = 256
SB_HEAD_GROUP = 4
REC_BLK = 64
VMEM_LIMIT = 56 * 1024 * 1024


def _cparams(n_axes):
    return pltpu.CompilerParams(dimension_semantics=("arbitrary",) * n_axes,
                                vmem_limit_bytes=VMEM_LIMIT)


def _ln_rows(x, g, b):
    mu = jnp.mean(x, axis=-1, keepdims=True)
    xc = x - mu
    var = jnp.mean(xc * xc, axis=-1, keepdims=True)
    return xc * lax.rsqrt(var + LN_EPS) * g + b


def _log1p_exp(neg):
    return jnp.log(1.0 + jnp.exp(neg))


def _log_sigmoid(z):
    return jnp.minimum(z, 0.0) - _log1p_exp(-jnp.abs(z))


def _sigmoid(z):
    return 1.0 / (1.0 + jnp.exp(-z))


def _dot(a, b):
    return jnp.dot(a, b, preferred_element_type=F32)


def _dot_nt(a, b):
    return lax.dot_general(a, b, (((1,), (1,)), ((), ())), preferred_element_type=F32)


def _dot_tn(a, b):
    return lax.dot_general(a, b, (((0,), (0,)), ((), ())), preferred_element_type=F32)


def _split_bf16(x):
    hi = x.astype(BF16)
    lo = (x - hi.astype(F32)).astype(BF16)
    return hi, lo


def _resident(shape):
    return pl.BlockSpec(shape, lambda *_: (0,) * len(shape), pipeline_mode=pl.Buffered(1))


def _ln_in_kernel(x_ref, g_ref, b_ref, xn_ref, xb_ref):
    x = _ln_rows(x_ref[...], g_ref[...], b_ref[...])
    xn_ref[...] = x
    xb_ref[...] = x.astype(BF16)


def _ln_in(x, g, b, *, tm):
    t, d = x.shape
    x_spec = pl.BlockSpec((tm, d), lambda i: (i, 0))
    vec_spec = pl.BlockSpec((1, d), lambda i: (0, 0))
    return pl.pallas_call(
        _ln_in_kernel, grid=(t // tm,),
        in_specs=[x_spec, vec_spec, vec_spec], out_specs=[x_spec, x_spec],
        out_shape=[jax.ShapeDtypeStruct((t, d), F32), jax.ShapeDtypeStruct((t, d), BF16)],
        compiler_params=_cparams(1), name="ln_in",
    )(x, g.reshape(1, d), b.reshape(1, d))


def _mm_kernel(x_ref, w_ref, o_ref):
    o_ref[...] = _dot(x_ref[...].astype(BF16), w_ref[...]).astype(o_ref.dtype)


def _mm(x, w, out_dtype, *, tm, tn):
    t, d = x.shape
    n = w.shape[1]
    return pl.pallas_call(
        _mm_kernel, grid=(t // tm, n // tn),
        in_specs=[pl.BlockSpec((tm, d), lambda i, j: (i, 0)),
                  pl.BlockSpec((d, tn), lambda i, j: (0, j))],
        out_specs=pl.BlockSpec((tm, tn), lambda i, j: (i, j)),
        out_shape=jax.ShapeDtypeStruct((t, n), out_dtype),
        compiler_params=_cparams(2), name="mm",
    )(x, w)


def _proj_qkv_kernel(x_ref, w_ref, *refs, layer, creates):
    k_nat_ref, v_nat_ref, qb_ref, kb_ref, vb_ref = refs[-5:]
    tm = x_ref.shape[0]
    res = _dot(x_ref[...], w_ref[...])
    qb_ref[...] = res[:, :SB_WIDTH].astype(BF16)
    kb_ref[...] = res[:, SB_WIDTH:2 * SB_WIDTH].astype(BF16)
    vb_ref[...] = res[:, 2 * SB_WIDTH:].astype(BF16)
    if creates:
        for l in range(k_nat_ref.shape[0]):
            if l != layer:
                k_nat_ref[l] = jnp.zeros(k_nat_ref.shape[1:], F32)
                v_nat_ref[l] = jnp.zeros(v_nat_ref.shape[1:], F32)
        k_dst, v_dst = k_nat_ref.at[layer], v_nat_ref.at[layer]
    else:
        k_dst, v_dst = k_nat_ref, v_nat_ref
    for h in range(SB_HEADS):
        rows = pl.ds(h, tm, stride=SB_HEADS)
        k_dst[rows, :] = res[:, SB_WIDTH + h * HEAD_DIM:SB_WIDTH + (h + 1) * HEAD_DIM]
        v_dst[rows, :] = res[:, 2 * SB_WIDTH + h * HEAD_DIM:2 * SB_WIDTH + (h + 1) * HEAD_DIM]


def _proj_qkv(xb, w, k_all, v_all, layer, depth, *, tm):
    t, d = xb.shape
    creates = k_all is None
    if creates:
        nat = pl.BlockSpec((depth, tm * SB_HEADS, HEAD_DIM), lambda i: (0, i, 0))
    else:
        nat = pl.BlockSpec((None, tm * SB_HEADS, HEAD_DIM), lambda i: (layer, i, 0))
    cols = pl.BlockSpec((tm, SB_WIDTH), lambda i: (i, 0))
    in_specs = [pl.BlockSpec((tm, d), lambda i: (i, 0)), _resident((d, 3 * SB_WIDTH))]
    args = [xb, w]
    aliases = {}
    if not creates:
        in_specs += [pl.BlockSpec(memory_space=pl.ANY)] * 2
        args += [k_all, v_all]
        aliases = {2: 0, 3: 1}
    nat_shape = jax.ShapeDtypeStruct((depth, t * SB_HEADS, HEAD_DIM), F32)
    cols_shape = jax.ShapeDtypeStruct((t, SB_WIDTH), BF16)
    return pl.pallas_call(
        functools.partial(_proj_qkv_kernel, layer=layer, creates=creates), grid=(t // tm,),
        in_specs=in_specs,
        out_specs=[nat, nat, cols, cols, cols],
        out_shape=[nat_shape, nat_shape, cols_shape, cols_shape, cols_shape],
        input_output_aliases=aliases,
        compiler_params=_cparams(1), name="proj_qkv",
    )(*args)


def _sb_scores(qb, kb, mask, ls_ref, lhs_ref, h):
    tq, tk = qb.shape[0], kb.shape[0]
    cw = min(SB_BLK, tk)
    z = _dot_nt(qb, kb) * ATT_SCALE
    ls = _log_sigmoid(z)
    lk = ls - z
    if mask is not None:
        lk = jnp.where(mask, lk, 0.0)
    hi, lo = _split_bf16(lk)
    ls_ref[h, :, :tk] = ls
    for c in range(tk // cw):
        cols = slice(c * cw, (c + 1) * cw)
        lhs_ref[h, c * tq:(c + 1) * tq, :2 * cw] = jnp.concatenate([hi[:, cols], lo[:, cols]], axis=1)


def _sb_weights(tk, mt2, mask, ls_ref, lhs_ref, w_ref, carry_ref, h, first):
    tq = ls_ref.shape[1]
    cw = min(SB_BLK, tk)
    nc = tk // cw
    rhs = mt2 if cw == SB_BLK else jnp.concatenate([mt2[:cw], mt2[SB_BLK:SB_BLK + cw]], axis=0)
    lt = _dot(lhs_ref[h, :nc * tq, :2 * cw], rhs)
    carry = jnp.zeros((tq, SB_BLK), F32) if first else carry_ref[h]
    for c in reversed(range(nc)):
        rows = slice(c * tq, (c + 1) * tq)
        cols = slice(c * cw, (c + 1) * cw)
        w = jnp.exp(ls_ref[h, :, cols] + lt[rows, :cw] + carry[:, :cw])
        if mask is not None:
            w = jnp.where(mask[:, cols], w, 0.0)
        w_ref[h, :, cols] = w.astype(BF16)
        carry = carry + lt[rows, SB_BLK:]
    carry_ref[h] = carry


def _sb_values(tk, vb, w_ref, acc_ref, h, first):
    pv = _dot(w_ref[h, :, :tk], vb)
    acc_ref[h] = pv if first else acc_ref[h] + pv


def _sb_add_span(q_of, k_of, v_of, n_heads, tk, mt2, mask, first, ls_ref, lhs_ref, w_ref,
                 carry_ref, acc_ref):
    for h in range(n_heads):
        _sb_scores(q_of(h), k_of(h), mask, ls_ref, lhs_ref, h)
    for h in range(n_heads):
        _sb_weights(tk, mt2, mask, ls_ref, lhs_ref, w_ref, carry_ref, h, first)
    for h in range(n_heads):
        _sb_values(tk, v_of(h), w_ref, acc_ref, h, first)


def _strict_lower(n):
    row = lax.broadcasted_iota(jnp.int32, (n, n), 0)
    col = lax.broadcasted_iota(jnp.int32, (n, n), 1)
    return col < row


def _sb_scratch(n_heads, tq, tk):
    return [pltpu.VMEM((n_heads, tq, SB_BLK), F32),
            pltpu.VMEM((n_heads, tq, HEAD_DIM), F32),
            pltpu.VMEM((n_heads, tq, tk), F32),
            pltpu.VMEM((n_heads, (tk // SB_BLK) * tq, 2 * SB_BLK), BF16),
            pltpu.VMEM((n_heads, tq, tk), BF16)]


def _sb_prompt_kernel(q_ref, k_ref, v_ref, mt_ref, o_ref, carry_ref, acc_ref, ls_ref, lhs_ref, w_ref):
    i = pl.program_id(2)
    tq = q_ref.shape[0]
    n_heads = q_ref.shape[1] // HEAD_DIM
    mt2 = mt_ref[...]
    heads = [slice(h * HEAD_DIM, (h + 1) * HEAD_DIM) for h in range(n_heads)]

    def add_span(rows, mask, first):
        _sb_add_span(lambda h: q_ref[:, heads[h]], lambda h: k_ref[rows, heads[h]],
                     lambda h: v_ref[rows, heads[h]], n_heads, tq, mt2, mask, first,
                     ls_ref, lhs_ref, w_ref, carry_ref, acc_ref)

    add_span(pl.ds(pl.multiple_of(i * tq, tq), tq), _strict_lower(tq), True)

    def body(s, c):
        add_span(pl.ds(pl.multiple_of((i - 1 - s) * tq, tq), tq), None, False)
        return c

    lax.fori_loop(0, i, body, 0)
    for h in range(n_heads):
        o_ref[:, heads[h]] = acc_ref[h].astype(o_ref.dtype)


def _sb_prompt(qb, kb, vb, mt2, batch, seq):
    tq = min(SB_TQ, seq)
    nq = seq // tq
    hg = SB_HEAD_GROUP
    gw = hg * HEAD_DIM
    q_spec = pl.BlockSpec((tq, gw), lambda b, g, i: (b * nq + i, g))
    kv_spec = pl.BlockSpec((seq, gw), lambda b, g, i: (b, g))
    return pl.pallas_call(
        _sb_prompt_kernel, grid=(batch, SB_HEADS // hg, nq),
        in_specs=[q_spec, kv_spec, kv_spec, _resident((2 * SB_BLK, 2 * SB_BLK))],
        out_specs=q_spec,
        out_shape=jax.ShapeDtypeStruct((batch * seq, SB_WIDTH), BF16),
        scratch_shapes=_sb_scratch(hg, tq, tq),
        compiler_params=_cparams(3), name="sb_prompt",
    )(qb, kb, vb, mt2)


def _sb_sample_kernel(q_ref, kn_ref, vn_ref, kp_ref, vp_ref, mt_ref, o_ref,
                      carry_ref, acc_ref, ls_ref, lhs_ref, w_ref):
    c = pl.program_id(1)
    tq = q_ref.shape[0]
    n_pos = kp_ref.shape[0] // SB_HEADS
    mt2 = mt_ref[...]
    heads = [slice(h * HEAD_DIM, (h + 1) * HEAD_DIM) for h in range(SB_HEADS)]
    scratch = (ls_ref, lhs_ref, w_ref, carry_ref, acc_ref)

    @pl.when(c == 0)
    def _():
        _sb_add_span(lambda h: q_ref[:, heads[h]], lambda h: kn_ref[:, heads[h]],
                     lambda h: vn_ref[:, heads[h]], SB_HEADS, tq, mt2, _strict_lower(tq), True,
                     *scratch)

    past_rows = lambda h: pl.ds(h, n_pos, stride=SB_HEADS)
    _sb_add_span(lambda h: q_ref[:, heads[h]], lambda h: kp_ref[past_rows(h), :].astype(BF16),
                 lambda h: vp_ref[past_rows(h), :].astype(BF16), SB_HEADS, n_pos, mt2, None, False,
                 *scratch)

    @pl.when(c == pl.num_programs(1) - 1)
    def _():
        for h in range(SB_HEADS):
            o_ref[:, heads[h]] = acc_ref[h].astype(o_ref.dtype)


def _sb_sample(qb, kb, vb, past_k, past_v, layer, mt2, batch, tq, *, chunk):
    past = past_k.shape[2] // SB_HEADS
    n_chunks = past // chunk
    new = pl.BlockSpec((tq, SB_WIDTH), lambda b, c: (b, 0))
    cache = pl.BlockSpec((None, None, chunk * SB_HEADS, HEAD_DIM),
                         lambda b, c: (layer, b, n_chunks - 1 - c, 0))
    return pl.pallas_call(
        _sb_sample_kernel, grid=(batch, n_chunks),
        in_specs=[new, new, new, cache, cache, _resident((2 * SB_BLK, 2 * SB_BLK))],
        out_specs=new,
        out_shape=jax.ShapeDtypeStruct((batch * tq, SB_WIDTH), BF16),
        scratch_shapes=_sb_scratch(SB_HEADS, tq, chunk),
        compiler_params=_cparams(2), name="sb_sample",
    )(qb, kb, vb, past_k, past_v, mt2)


def _rec_blocks(tc):
    return [(n, h) for n in range(tc // REC_BLK) for h in range(REC_HEADS)]


def _rec_rows(n):
    return slice(n * REC_BLK, (n + 1) * REC_BLK)


def _rec_cols(h):
    return slice(h * HEAD_DIM, (h + 1) * HEAD_DIM)


def _rec_prepare(q, k, v, log_g, lbd):
    hi, lo = _split_bf16(log_g)
    cum = _dot(lbd, hi) + _dot(lbd, lo)
    row = lax.broadcasted_iota(jnp.int32, (REC_BLK, REC_BLK), 0)
    col = lax.broadcasted_iota(jnp.int32, (REC_BLK, REC_BLK), 1)
    causal = col <= row
    half = REC_BLK // 2
    prep = {}
    for n, h in _rec_blocks(q.shape[0]):
        rs, cs = _rec_rows(n), _rec_cols(h)
        c = cum[rs, cs]
        mid = c[half - 1:half, :]
        last = c[REC_BLK - 1:REC_BLK, :]
        q1 = q[rs, cs] * jnp.exp(c - mid)
        k1 = k[rs, cs] * jnp.exp(mid - c)
        vb = v[rs, cs].astype(BF16)
        s = jnp.where(causal, _dot_nt(q1.astype(BF16), k1.astype(BF16)), 0.0)
        o_intra = _dot(s.astype(BF16), vb)
        qd = (q1 * jnp.exp(mid)).astype(BF16)
        ds = _dot_tn(vb, (k1 * jnp.exp(last - mid)).astype(BF16))
        prep[n, h] = (o_intra, qd, ds, jnp.exp(last))
    return prep


def _rec_chain(prep, gate, norm_g, st_ref, o_ref):
    for n, h in _rec_blocks(gate.shape[0]):
        rs, cs = _rec_rows(n), _rec_cols(h)
        o_intra, qd, ds, decay = prep[n, h]
        st = st_ref[h]
        o = o_intra + _dot_nt(qd, st.astype(BF16))
        st_ref[h] = st * decay + ds
        ms = jnp.mean(o * o, axis=-1, keepdims=True)
        o = o * lax.rsqrt(ms + LN_EPS) * norm_g
        g = gate[rs, cs]
        o_ref[rs, cs] = (o * (g * _sigmoid(g))).astype(o_ref.dtype)


def _rec_kernel(gq_ref, gk_ref, gv_ref, gg_ref, ga_ref, hf_ref, hq_ref, hi_ref, ho_ref,
                wa_ref, ba_ref, gla_ng_ref, lb_ref, hg_ng_ref, lbd_ref, s0g_ref, s0h_ref,
                og_ref, oh_ref, sg_ref, sh_ref, stg_ref, sth_ref):
    c = pl.program_id(1)

    @pl.when(c == 0)
    def _():
        for h in range(REC_HEADS):
            stg_ref[h] = s0g_ref[0, h].T
            sth_ref[h] = s0h_ref[0, h].T

    lbd = lbd_ref[...]
    pre = _dot(ga_ref[...].astype(BF16), wa_ref[...]) + ba_ref[...]
    log_g = _log_sigmoid(pre) * (1.0 / GLA_GATE_NORM)
    prep_g = _rec_prepare(gq_ref[...] * ATT_SCALE, gk_ref[...], gv_ref[...], log_g, lbd)

    lb = lb_ref[...]
    zf = hf_ref[...]
    a = jnp.log(lb)
    b = jnp.log(1.0 - lb) + _log_sigmoid(zf)
    log_f = jnp.maximum(a, b) + _log1p_exp(-jnp.abs(a - b))
    k_h = (1.0 - lb) * _sigmoid(-zf)
    hq = hq_ref[...]
    q_h = hq * _sigmoid(hq) * ATT_SCALE
    prep_h = _rec_prepare(q_h, k_h, hi_ref[...], log_f, lbd)

    _rec_chain(prep_g, gg_ref[...], gla_ng_ref[...], stg_ref, og_ref)
    _rec_chain(prep_h, ho_ref[...], hg_ng_ref[...], sth_ref, oh_ref)

    @pl.when(c == pl.num_programs(1) - 1)
    def _():
        for h in range(REC_HEADS):
            sg_ref[0, h] = stg_ref[h].T
            sh_ref[0, h] = sth_ref[h].T


def _rec(rec, wa, ba, gla_ng, lb, hg_ng, lbd, s_gla, s_hg, s_layer, batch, t_b, tc):
    nc = t_b // tc
    col = lambda c0: pl.BlockSpec((tc, REC_WIDTH), lambda b, c, c0=c0: (b * nc + c, c0 // REC_WIDTH))
    full = lambda shape: pl.BlockSpec(shape, lambda b, c: (0,) * len(shape))
    a_spec = pl.BlockSpec((tc, HEAD_DIM), lambda b, c: (b * nc + c, R_GA // HEAD_DIM))
    state_in = pl.BlockSpec((None, 1, REC_HEADS, HEAD_DIM, HEAD_DIM),
                            lambda b, c: (s_layer, b, 0, 0, 0))
    state_out = pl.BlockSpec((1, REC_HEADS, HEAD_DIM, HEAD_DIM), lambda b, c: (b, 0, 0, 0))
    o_spec = pl.BlockSpec((tc, REC_WIDTH), lambda b, c: (b * nc + c, 0))
    o_shape = jax.ShapeDtypeStruct((batch * t_b, REC_WIDTH), BF16)
    s_shape = jax.ShapeDtypeStruct((batch, REC_HEADS, HEAD_DIM, HEAD_DIM), F32)
    state_scratch = pltpu.VMEM((REC_HEADS, HEAD_DIM, HEAD_DIM), F32)
    return pl.pallas_call(
        _rec_kernel, grid=(batch, nc),
        in_specs=[col(R_GQ), col(R_GK), col(R_GV), col(R_GG), a_spec,
                  col(R_HF), col(R_HQ), col(R_HI), col(R_HO),
                  full((HEAD_DIM, REC_WIDTH)), full((1, REC_WIDTH)), full((1, HEAD_DIM)),
                  full((1, REC_WIDTH)), full((1, HEAD_DIM)), full((tc, tc)), state_in, state_in],
        out_specs=[o_spec, o_spec, state_out, state_out],
        out_shape=[o_shape, o_shape, s_shape, s_shape],
        scratch_shapes=[state_scratch, state_scratch],
        compiler_params=_cparams(2), name="rec",
    )(rec, rec, rec, rec, rec, rec, rec, rec, rec, wa, ba, gla_ng, lb, hg_ng, lbd, s_gla, s_hg)


def _mix_kernel(x_ref, ga_ref, gb_ref, gc_ref, oa_ref, ob_ref, oc_ref,
                wsb_ref, wgla_ref, whg_ref, wo_ref, g_ref, b_ref, o_ref):
    m = _sigmoid(ga_ref[...]) * _dot(oa_ref[...], wsb_ref[...])
    m = m + _sigmoid(gb_ref[...]) * _dot(ob_ref[...], wgla_ref[...])
    m = m + _sigmoid(gc_ref[...]) * _dot(oc_ref[...], whg_ref[...])
    y = ALPHA * x_ref[...] + _dot(m.astype(BF16), wo_ref[...])
    o_ref[...] = _ln_rows(y, g_ref[...], b_ref[...])


def _mix(x, gates, oa, ob, oc, wsb, wgla, whg, wo, g, b, *, tm):
    t, d = x.shape
    row = lambda w, j=0: pl.BlockSpec((tm, w), lambda i, j=j: (i, j))
    return pl.pallas_call(
        _mix_kernel, grid=(t // tm,),
        in_specs=[row(d), row(d, 0), row(d, 1), row(d, 2),
                  row(SB_WIDTH), row(REC_WIDTH), row(REC_WIDTH),
                  _resident((SB_WIDTH, d)), _resident((REC_WIDTH, d)), _resident((REC_WIDTH, d)),
                  _resident((d, d)), _resident((1, d)), _resident((1, d))],
        out_specs=row(d),
        out_shape=jax.ShapeDtypeStruct((t, d), F32),
        compiler_params=_cparams(1), name="mix",
    )(x, gates, gates, gates, oa, ob, oc, wsb, wgla, whg, wo, g.reshape(1, d), b.reshape(1, d))


def _xattn_kernel(x_ref, mk_ref, mv_ref, wq_ref, wo_ref, g_ref, b_ref, o_ref):
    x = x_ref[...]
    n_b = mk_ref.shape[0]
    rows_b = x.shape[0] // n_b
    q = _dot(x.astype(BF16), wq_ref[...]).astype(BF16)
    pairs = [(r, h) for r in range(n_b) for h in range(XA_HEADS)]
    rows = lambda r: slice(r * rows_b, (r + 1) * rows_b)
    cols = lambda h: slice(h * HEAD_DIM, (h + 1) * HEAD_DIM)
    if len(mk_ref.shape) == 4:
        mem_head = lambda ref, r, h: ref[r, :, h, :].astype(BF16)
    else:
        mem_head = lambda ref, r, h: ref[r, :, cols(h)].astype(BF16)
    scores = {(r, h): _dot_nt(q[rows(r), cols(h)], mem_head(mk_ref, r, h)) for r, h in pairs}
    probs = {}
    for p in pairs:
        s = scores[p] * ATT_SCALE
        e = jnp.exp(s - jnp.max(s, axis=-1, keepdims=True))
        probs[p] = (e / jnp.sum(e, axis=-1, keepdims=True)).astype(BF16)
    outs = {(r, h): _dot(probs[r, h], mem_head(mv_ref, r, h)).astype(BF16) for r, h in pairs}
    per_batch = [jnp.concatenate([outs[r, h] for h in range(XA_HEADS)], axis=-1)
                 for r in range(n_b)]
    o = per_batch[0] if n_b == 1 else jnp.concatenate(per_batch, axis=0)
    y = ALPHA * x + _dot(o, wo_ref[...])
    o_ref[...] = _ln_rows(y, g_ref[...], b_ref[...])


def _xattn(x, mem_k, mem_v, wq, wo, g, b, *, batch, tm, mem_layer=None):
    t, d = x.shape
    t_b = t // batch
    if tm <= t_b:
        n_b, nt = 1, t_b // tm
        grid = (batch, nt)
        row = pl.BlockSpec((tm, d), lambda bi, i: (bi * nt + i, 0))
    else:
        n_b = tm // t_b
        grid = (batch // n_b, 1)
        row = pl.BlockSpec((tm, d), lambda bi, i: (bi, 0))
    if mem_layer is None:
        mem = pl.BlockSpec((n_b,) + mem_k.shape[1:], lambda bi, i: (bi, 0, 0))
    else:
        mem = pl.BlockSpec((None, n_b) + mem_k.shape[2:], lambda bi, i: (mem_layer, bi, 0, 0, 0))
    return pl.pallas_call(
        _xattn_kernel, grid=grid,
        in_specs=[row, mem, mem, _resident((d, XA_WIDTH)), _resident((XA_WIDTH, d)),
                  _resident((1, d)), _resident((1, d))],
        out_specs=row,
        out_shape=jax.ShapeDtypeStruct((t, d), F32),
        compiler_params=_cparams(2), name="xattn",
    )(x, mem_k, mem_v, wq, wo, g.reshape(1, d), b.reshape(1, d))


def _ffn_kernel(x_ref, wg_ref, wu_ref, wd_ref, g_ref, b_ref, o_ref, ob_ref, xs_ref, acc_ref):
    f = pl.program_id(1)

    @pl.when(f == 0)
    def _():
        xs_ref[...] = x_ref[...].astype(BF16)
        acc_ref[...] = jnp.zeros_like(acc_ref)

    xs = xs_ref[...]
    gate = _dot(xs, wg_ref[...])
    up = _dot(xs, wu_ref[...])
    hidden = gate * _sigmoid(gate) * up
    acc_ref[...] += _dot(hidden.astype(BF16), wd_ref[...])

    @pl.when(f == pl.num_programs(1) - 1)
    def _():
        y = _ln_rows(ALPHA * x_ref[...] + acc_ref[...], g_ref[...], b_ref[...])
        o_ref[...] = y
        ob_ref[...] = y.astype(BF16)


def _ffn(x, w_in, w_out, g, b, *, tm, tf):
    t, d = x.shape
    nf = D_FF // tf
    row = pl.BlockSpec((tm, d), lambda i, f: (i, 0))
    vec = pl.BlockSpec((1, d), lambda i, f: (0, 0))
    return pl.pallas_call(
        _ffn_kernel, grid=(t // tm, nf),
        in_specs=[row,
                  pl.BlockSpec((d, tf), lambda i, f: (0, f)),
                  pl.BlockSpec((d, tf), lambda i, f: (0, nf + f)),
                  pl.BlockSpec((tf, d), lambda i, f: (f, 0)),
                  vec, vec],
        out_specs=[row, row],
        out_shape=[jax.ShapeDtypeStruct((t, d), F32), jax.ShapeDtypeStruct((t, d), BF16)],
        scratch_shapes=[pltpu.VMEM((tm, d), BF16), pltpu.VMEM((tm, d), F32)],
        compiler_params=_cparams(2), name="ffn",
    )(x, w_in, w_in, w_out, g.reshape(1, d), b.reshape(1, d))


def _tri_tables():
    j = lax.broadcasted_iota(jnp.int32, (SB_BLK, SB_BLK), 0)
    s = lax.broadcasted_iota(jnp.int32, (SB_BLK, SB_BLK), 1)
    mt = jnp.concatenate([(j > s).astype(BF16), jnp.ones((SB_BLK, SB_BLK), BF16)], axis=1)
    return jnp.concatenate([mt, mt], axis=0)


def _block_cumsum_table(tc):
    r = lax.broadcasted_iota(jnp.int32, (tc, tc), 0)
    c = lax.broadcasted_iota(jnp.int32, (tc, tc), 1)
    return ((c <= r) & (r // REC_BLK == c // REC_BLK)).astype(BF16)


def _layer(x, xb, lw, layer, mem_k, mem_v, s_gla, s_hg, s_layer, past, ln_in, k_all, v_all, depth,
           *, batch, t_b, cfg):
    tm = cfg["tm_proj"]
    if ln_in is not None:
        x, xb = _ln_in(x, ln_in[0], ln_in[1], tm=cfg["tm_ln"])
    gates = _mm(xb, lw["w_gates"], F32, tm=tm, tn=2048)
    rec = _mm(xb, lw["w_rec"], F32, tm=tm, tn=N_REC // 3)
    k_all, v_all, qb, kb, vb = _proj_qkv(xb, lw["w_sqkv"], k_all, v_all, layer, depth,
                                         tm=cfg["tm_kv"])
    if past is None:
        oa = _sb_prompt(qb, kb, vb, lw["mt2"], batch, t_b)
    else:
        oa = _sb_sample(qb, kb, vb, past[0], past[1], layer, lw["mt2"], batch, t_b,
                        chunk=cfg["sb_chunk"])
    tc = cfg["tc"]
    lbd = _block_cumsum_table(tc)
    ob, oc, s_gla_new, s_hg_new = _rec(rec, lw["wa"], lw["ba"], lw["gla_ng"], lw["lb"], lw["hg_ng"],
                                       lbd, s_gla, s_hg, s_layer, batch, t_b, tc)
    x = _mix(x, gates, oa, ob, oc, lw["w_up_sb"], lw["w_up_gla"], lw["w_up_hg"], lw["w_o"],
             lw["ln1_g"], lw["ln1_b"], tm=cfg["tm_mix"])
    x = _xattn(x, mem_k, mem_v, lw["w_xq"], lw["w_xo"], lw["ln2_g"], lw["ln2_b"],
               batch=batch, tm=cfg["tm_xa"], mem_layer=None if past is None else layer)
    x, xb = _ffn(x, lw["w_ffn_in"], lw["w_ffn_out"], lw["ln3_g"], lw["ln3_b"],
                 tm=cfg["tm_ffn"], tf=cfg["tf"])
    return x, xb, k_all, v_all, s_gla_new, s_hg_new


def _tile(n, pref):
    t = min(n, pref)
    while n % t:
        t //= 2
    return t


def kernel(x_prompt, x_sample, mem_prompt, cache_sb_k, cache_sb_v, cache_mem_k, cache_mem_v,
           state_gla, state_hgrn, ln_in_g, ln_in_b, w_in, gla_w_a, gla_b_a, gla_norm_g, hg_lower,
           hg_norm_g, w_up_sb, w_up_gla, w_up_hg, w_o, ln1_g, ln1_b, w_xq, w_xk, w_xv, w_xo,
           ln2_g, ln2_b, w_ffn_in, w_ffn_out, ln3_g, ln3_b):
    bp, seq, d = x_prompt.shape
    bs, tq, _ = x_sample.shape
    n_mem = mem_prompt.shape[1]
    past_len = cache_sb_k.shape[2]
    depth = w_in.shape[0]

    lb_all = jnp.cumsum(jax.nn.softmax(hg_lower.astype(F32), axis=0), axis=0)
    lb_all = lb_all - lb_all[0:1]
    mt2 = _tri_tables()
    wa_pad = jnp.zeros((depth, HEAD_DIM, REC_WIDTH), BF16).at[:, :GLA_RANK].set(gla_w_a.astype(BF16))
    ga_zero = jnp.zeros((d, GA_PAD - GLA_RANK), BF16)

    xp = x_prompt.reshape(bp * seq, d)
    xs = x_sample.reshape(bs * tq, d)
    xpb = xsb = None
    mem2d = mem_prompt.reshape(bp * n_mem, d)
    zero_state = jnp.zeros((1, bp, REC_HEADS, HEAD_DIM, HEAD_DIM), F32)
    kp_all = vp_all = ks_all = vs_all = None
    past = (cache_sb_k.reshape(depth, bs, past_len * SB_HEADS, HEAD_DIM),
            cache_sb_v.reshape(depth, bs, past_len * SB_HEADS, HEAD_DIM))

    cfg_p = dict(tm_proj=_tile(bp * seq, 1024), tm_kv=_tile(bp * seq, 512), tc=_tile(seq, 512),
                 tm_mix=_tile(bp * seq, 256), tm_xa=_tile(seq, 512),
                 tm_ffn=_tile(bp * seq, 512), tf=512, tm_ln=_tile(bp * seq, 256))
    cfg_s = dict(tm_proj=_tile(bs * tq, 1024), tm_kv=_tile(bs * tq, 512), tc=_tile(tq, 256),
                 tm_mix=_tile(bs * tq, 256), tm_xa=_tile(bs * tq, 256),
                 tm_ffn=_tile(bs * tq, 512), tf=512, tm_ln=_tile(bs * tq, 256),
                 sb_chunk=_tile(past_len, 2048))

    outs_p, outs_s = [], []
    for l in range(depth):
        w = w_in[l]
        lw = dict(
            w_gates=w[:, W_GATES:].astype(BF16), w_sqkv=w[:, W_SQ:W_REC_A].astype(BF16),
            w_rec=jnp.concatenate([w[:, W_REC_A:W_GA].astype(BF16), w[:, W_REC_H:W_GATES].astype(BF16),
                                   w[:, W_GA:W_REC_H].astype(BF16), ga_zero], axis=1),
            mt2=mt2, wa=wa_pad[l], ba=gla_b_a[l].reshape(1, REC_WIDTH),
            gla_ng=gla_norm_g[l].reshape(1, HEAD_DIM), lb=lb_all[l].reshape(1, REC_WIDTH),
            hg_ng=hg_norm_g[l].reshape(1, HEAD_DIM),
            w_up_sb=w_up_sb[l].astype(BF16), w_up_gla=w_up_gla[l].astype(BF16),
            w_up_hg=w_up_hg[l].astype(BF16), w_o=w_o[l].astype(BF16),
            ln1_g=ln1_g[l], ln1_b=ln1_b[l], w_xq=w_xq[l].astype(BF16), w_xo=w_xo[l].astype(BF16),
            ln2_g=ln2_g[l], ln2_b=ln2_b[l], w_ffn_in=w_ffn_in[l].astype(BF16),
            w_ffn_out=w_ffn_out[l].astype(BF16), ln3_g=ln3_g[l], ln3_b=ln3_b[l])
        ln_in = (ln_in_g, ln_in_b) if l == 0 else None
        tm_mem = _tile(bp * n_mem, 512)
        mem_k = _mm(mem2d, w_xk[l].astype(BF16), F32, tm=tm_mem, tn=XA_WIDTH)
        mem_v = _mm(mem2d, w_xv[l].astype(BF16), F32, tm=tm_mem, tn=XA_WIDTH)
        xp, xpb, kp_all, vp_all, gp, hp = _layer(
            xp, xpb, lw, l, mem_k.reshape(bp, n_mem, XA_WIDTH), mem_v.reshape(bp, n_mem, XA_WIDTH),
            zero_state, zero_state, 0, None, ln_in, kp_all, vp_all, depth,
            batch=bp, t_b=seq, cfg=cfg_p)
        xs, xsb, ks_all, vs_all, gs, hs = _layer(
            xs, xsb, lw, l, cache_mem_k, cache_mem_v,
            state_gla, state_hgrn, l, past, ln_in, ks_all, vs_all, depth,
            batch=bs, t_b=tq, cfg=cfg_s)
        outs_p.append((mem_k.reshape(bp, n_mem, XA_HEADS, HEAD_DIM),
                       mem_v.reshape(bp, n_mem, XA_HEADS, HEAD_DIM), gp, hp))
        outs_s.append((gs, hs))
    stk = lambda outs, i: jnp.stack([o[i] for o in outs], axis=0)
    kv_p = (depth, bp, seq, SB_HEADS, HEAD_DIM)
    kv_s = (depth, bs, tq, SB_HEADS, HEAD_DIM)
    return (xp.reshape(bp, seq, d), xs.reshape(bs, tq, d),
            kp_all.reshape(kv_p), vp_all.reshape(kv_p), stk(outs_p, 0), stk(outs_p, 1),
            stk(outs_p, 2), stk(outs_p, 3), ks_all.reshape(kv_s), vs_all.reshape(kv_s),
            stk(outs_s, 0), stk(outs_s, 1))
```

```python
import functools

import jax
import jax.numpy as jnp
from jax import lax
from jax.experimental import pallas as pl
from jax.experimental.pallas import tpu as pltpu

F32 = jnp.float32
BF16 = jnp.bfloat16

D_MODEL = 2048
DEPTH = 2
HEAD_DIM = 128
SB_HEADS = 8
SB_WIDTH = SB_HEADS * HEAD_DIM
REC_HEADS = 4
REC_WIDTH = REC_HEADS * HEAD_DIM
GLA_RANK = 16
GLA_GATE_NORM = 16.0
XA_HEADS = 4
XA_WIDTH = XA_HEADS * HEAD_DIM
D_FF = 5632
LN_EPS = 1e-5
ALPHA = (2 * DEPTH) ** 0.25
ATT_SCALE = HEAD_DIM ** -0.5

W_SQ = 0
W_REC_A = 3 * SB_WIDTH
W_GA = W_REC_A + 4 * REC_WIDTH
W_REC_H = W_GA + GLA_RANK
W_GATES = W_REC_H + 4 * REC_WIDTH

GA_PAD = HEAD_DIM
R_GQ, R_GK, R_GV, R_GG, R_HF, R_HQ, R_HI, R_HO, R_GA = (i * REC_WIDTH for i in range(9))
N_REC = R_GA + GA_PAD

SB_BLK = 128
SB_TQ = 256
SB_HEAD_GROUP = 4
REC_BLK = 64
VMEM_LIMIT = 56 * 1024 * 1024


def _cparams(n_axes):
    return pltpu.CompilerParams(dimension_semantics=("arbitrary",) * n_axes,
                                vmem_limit_bytes=VMEM_LIMIT)


def _ln_rows(x, g, b):
    mu = jnp.mean(x, axis=-1, keepdims=True)
    xc = x - mu
    var = jnp.mean(xc * xc, axis=-1, keepdims=True)
    return xc * lax.rsqrt(var + LN_EPS) * g + b


def _log1p_exp(neg):
    return jnp.log(1.0 + jnp.exp(neg))


def _log_sigmoid(z):
    return jnp.minimum(z, 0.0) - _log1p_exp(-jnp.abs(z))


def _sigmoid(z):
    return 1.0 / (1.0 + jnp.exp(-z))


def _dot(a, b):
    return jnp.dot(a, b, preferred_element_type=F32)


def _dot_nt(a, b):
    return lax.dot_general(a, b, (((1,), (1,)), ((), ())), preferred_element_type=F32)


def _dot_tn(a, b):
    return lax.dot_general(a, b, (((0,), (0,)), ((), ())), preferred_element_type=F32)


def _split_bf16(x):
    hi = x.astype(BF16)
    lo = (x - hi.astype(F32)).astype(BF16)
    return hi, lo


def _resident(shape):
    return pl.BlockSpec(shape, lambda *_: (0,) * len(shape), pipeline_mode=pl.Buffered(1))


def _ln_in_kernel(x_ref, g_ref, b_ref, xn_ref, xb_ref):
    x = _ln_rows(x_ref[...], g_ref[...], b_ref[...])
    xn_ref[...] = x
    xb_ref[...] = x.astype(BF16)


def _ln_in(x, g, b, *, tm):
    t, d = x.shape
    x_spec = pl.BlockSpec((tm, d), lambda i: (i, 0))
    vec_spec = pl.BlockSpec((1, d), lambda i: (0, 0))
    return pl.pallas_call(
        _ln_in_kernel, grid=(t // tm,),
        in_specs=[x_spec, vec_spec, vec_spec], out_specs=[x_spec, x_spec],
        out_shape=[jax.ShapeDtypeStruct((t, d), F32), jax.ShapeDtypeStruct((t, d), BF16)],
        compiler_params=_cparams(1), name="ln_in",
    )(x, g.reshape(1, d), b.reshape(1, d))


def _mm_kernel(x_ref, w_ref, o_ref):
    o_ref[...] = _dot(x_ref[...].astype(BF16), w_ref[...]).astype(o_ref.dtype)


def _mm(x, w, out_dtype, *, tm, tn):
    t, d = x.shape
    n = w.shape[1]
    return pl.pallas_call(
        _mm_kernel, grid=(t // tm, n // tn),
        in_specs=[pl.BlockSpec((tm, d), lambda i, j: (i, 0)),
                  pl.BlockSpec((d, tn), lambda i, j: (0, j))],
        out_specs=pl.BlockSpec((tm, tn), lambda i, j: (i, j)),
        out_shape=jax.ShapeDtypeStruct((t, n), out_dtype),
        compiler_params=_cparams(2), name="mm",
    )(x, w)


def _proj_qkv_kernel(x_ref, w_ref, *refs, layer, creates):
    k_nat_ref, v_nat_ref, qb_ref, kb_ref, vb_ref = refs[-5:]
    tm = x_ref.shape[0]
    res = _dot(x_ref[...], w_ref[...])
    qb_ref[...] = res[:, :SB_WIDTH].astype(BF16)
    kb_ref[...] = res[:, SB_WIDTH:2 * SB_WIDTH].astype(BF16)
    vb_ref[...] = res[:, 2 * SB_WIDTH:].astype(BF16)
    if creates:
        for l in range(k_nat_ref.shape[0]):
            if l != layer:
                k_nat_ref[l] = jnp.zeros(k_nat_ref.shape[1:], F32)
                v_nat_ref[l] = jnp.zeros(v_nat_ref.shape[1:], F32)
        k_dst, v_dst = k_nat_ref.at[layer], v_nat_ref.at[layer]
    else:
        k_dst, v_dst = k_nat_ref, v_nat_ref
    for h in range(SB_HEADS):
        rows = pl.ds(h, tm, stride=SB_HEADS)
        k_dst[rows, :] = res[:, SB_WIDTH + h * HEAD_DIM:SB_WIDTH + (h + 1) * HEAD_DIM]
        v_dst[rows, :] = res[:, 2 * SB_WIDTH + h * HEAD_DIM:2 * SB_WIDTH + (h + 1) * HEAD_DIM]


def _proj_qkv(xb, w, k_all, v_all, layer, depth, *, tm):
    t, d = xb.shape
    creates = k_all is None
    if creates:
        nat = pl.BlockSpec((depth, tm * SB_HEADS, HEAD_DIM), lambda i: (0, i, 0))
    else:
        nat = pl.BlockSpec((None, tm * SB_HEADS, HEAD_DIM), lambda i: (layer, i, 0))
    cols = pl.BlockSpec((tm, SB_WIDTH), lambda i: (i, 0))
    in_specs = [pl.BlockSpec((tm, d), lambda i: (i, 0)), _resident((d, 3 * SB_WIDTH))]
    args = [xb, w]
    aliases = {}
    if not creates:
        in_specs += [pl.BlockSpec(memory_space=pl.ANY)] * 2
        args += [k_all, v_all]
        aliases = {2: 0, 3: 1}
    nat_shape = jax.ShapeDtypeStruct((depth, t * SB_HEADS, HEAD_DIM), F32)
    cols_shape = jax.ShapeDtypeStruct((t, SB_WIDTH), BF16)
    return pl.pallas_call(
        functools.partial(_proj_qkv_kernel, layer=layer, creates=creates), grid=(t // tm,),
        in_specs=in_specs,
        out_specs=[nat, nat, cols, cols, cols],
        out_shape=[nat_shape, nat_shape, cols_shape, cols_shape, cols_shape],
        input_output_aliases=aliases,
        compiler_params=_cparams(1), name="proj_qkv",
    )(*args)


def _sb_scores(qb, kb, mask, ls_ref, lhs_ref, h):
    tq, tk = qb.shape[0], kb.shape[0]
    cw = min(SB_BLK, tk)
    z = _dot_nt(qb, kb) * ATT_SCALE
    ls = _log_sigmoid(z)
    lk = ls - z
    if mask is not None:
        lk = jnp.where(mask, lk, 0.0)
    hi, lo = _split_bf16(lk)
    ls_ref[h, :, :tk] = ls
    for c in range(tk // cw):
        cols = slice(c * cw, (c + 1) * cw)
        lhs_ref[h, c * tq:(c + 1) * tq, :2 * cw] = jnp.concatenate([hi[:, cols], lo[:, cols]], axis=1)


def _sb_weights(tk, mt2, mask, ls_ref, lhs_ref, w_ref, carry_ref, h, first):
    tq = ls_ref.shape[1]
    cw = min(SB_BLK, tk)
    nc = tk // cw
    rhs = mt2 if cw == SB_BLK else jnp.concatenate([mt2[:cw], mt2[SB_BLK:SB_BLK + cw]], axis=0)
    lt = _dot(lhs_ref[h, :nc * tq, :2 * cw], rhs)
    carry = jnp.zeros((tq, SB_BLK), F32) if first else carry_ref[h]
    for c in reversed(range(nc)):
        rows = slice(c * tq, (c + 1) * tq)
        cols = slice(c * cw, (c + 1) * cw)
        w = jnp.exp(ls_ref[h, :, cols] + lt[rows, :cw] + carry[:, :cw])
        if mask is not None:
            w = jnp.where(mask[:, cols], w, 0.0)
        w_ref[h, :, cols] = w.astype(BF16)
        carry = carry + lt[rows, SB_BLK:]
    carry_ref[h] = carry


def _sb_values(tk, vb, w_ref, acc_ref, h, first):
    pv = _dot(w_ref[h, :, :tk], vb)
    acc_ref[h] = pv if first else acc_ref[h] + pv


def _sb_add_span(q_of, k_of, v_of, n_heads, tk, mt2, mask, first, ls_ref, lhs_ref, w_ref,
                 carry_ref, acc_ref):
    for h in range(n_heads):
        _sb_scores(q_of(h), k_of(h), mask, ls_ref, lhs_ref, h)
    for h in range(n_heads):
        _sb_weights(tk, mt2, mask, ls_ref, lhs_ref, w_ref, carry_ref, h, first)
    for h in range(n_heads):
        _sb_values(tk, v_of(h), w_ref, acc_ref, h, first)


def _strict_lower(n):
    row = lax.broadcasted_iota(jnp.int32, (n, n), 0)
    col = lax.broadcasted_iota(jnp.int32, (n, n), 1)
    return col < row


def _sb_scratch(n_heads, tq, tk):
    return [pltpu.VMEM((n_heads, tq, SB_BLK), F32),
            pltpu.VMEM((n_heads, tq, HEAD_DIM), F32),
            pltpu.VMEM((n_heads, tq, tk), F32),
            pltpu.VMEM((n_heads, (tk // SB_BLK) * tq, 2 * SB_BLK), BF16),
            pltpu.VMEM((n_heads, tq, tk), BF16)]


def _sb_prompt_kernel(q_ref, k_ref, v_ref, mt_ref, o_ref, carry_ref, acc_ref, ls_ref, lhs_ref, w_ref):
    i = pl.program_id(2)
    tq = q_ref.shape[0]
    n_heads = q_ref.shape[1] // HEAD_DIM
    mt2 = mt_ref[...]
    heads = [slice(h * HEAD_DIM, (h + 1) * HEAD_DIM) for h in range(n_heads)]

    def add_span(rows, mask, first):
        _sb_add_span(lambda h: q_ref[:, heads[h]], lambda h: k_ref[rows, heads[h]],
                     lambda h: v_ref[rows, heads[h]], n_heads, tq, mt2, mask, first,
                     ls_ref, lhs_ref, w_ref, carry_ref, acc_ref)

    add_span(pl.ds(pl.multiple_of(i * tq, tq), tq), _strict_lower(tq), True)

    def body(s, c):
        add_span(pl.ds(pl.multiple_of((i - 1 - s) * tq, tq), tq), None, False)
        return c

    lax.fori_loop(0, i, body, 0)
    for h in range(n_heads):
        o_ref[:, heads[h]] = acc_ref[h].astype(o_ref.dtype)


def _sb_prompt(qb, kb, vb, mt2, batch, seq):
    tq = min(SB_TQ, seq)
    nq = seq // tq
    hg = SB_HEAD_GROUP
    gw = hg * HEAD_DIM
    q_spec = pl.BlockSpec((tq, gw), lambda b, g, i: (b * nq + i, g))
    kv_spec = pl.BlockSpec((seq, gw), lambda b, g, i: (b, g))
    return pl.pallas_call(
        _sb_prompt_kernel, grid=(batch, SB_HEADS // hg, nq),
        in_specs=[q_spec, kv_spec, kv_spec, _resident((2 * SB_BLK, 2 * SB_BLK))],
        out_specs=q_spec,
        out_shape=jax.ShapeDtypeStruct((batch * seq, SB_WIDTH), BF16),
        scratch_shapes=_sb_scratch(hg, tq, tq),
        compiler_params=_cparams(3), name="sb_prompt",
    )(qb, kb, vb, mt2)


SB_RING = 3


def _sb_sample_kernel(q_ref, kn_ref, vn_ref, kp_hbm, vp_hbm, mt_ref, o_ref,
                      carry_ref, acc_ref, ls_ref, lhs_ref, w_ref, kbuf, vbuf, sem, *, layer):
    b, c = pl.program_id(0), pl.program_id(1)
    n_chunks = pl.num_programs(1)
    n_steps = pl.num_programs(0) * n_chunks
    t = b * n_chunks + c
    tq = q_ref.shape[0]
    rows = kbuf.shape[1]
    n_pos = rows // SB_HEADS
    mt2 = mt_ref[...]
    heads = [slice(h * HEAD_DIM, (h + 1) * HEAD_DIM) for h in range(SB_HEADS)]
    scratch = (ls_ref, lhs_ref, w_ref, carry_ref, acc_ref)

    def copies(u):
        bu, cu = u // n_chunks, u % n_chunks
        src = pl.ds(pl.multiple_of((n_chunks - 1 - cu) * rows, rows), rows)
        slot = u % SB_RING
        return (pltpu.make_async_copy(kp_hbm.at[layer, bu, src, :], kbuf.at[slot], sem.at[0, slot]),
                pltpu.make_async_copy(vp_hbm.at[layer, bu, src, :], vbuf.at[slot], sem.at[1, slot]))

    def start(u):
        for cp in copies(u):
            cp.start()

    @pl.when(t == 0)
    def _():
        start(t)

    @pl.when(jnp.logical_and(t == 0, n_steps > 1))
    def _():
        start(t + 1)

    @pl.when(t + 2 < n_steps)
    def _():
        start(t + 2)

    @pl.when(c == 0)
    def _():
        _sb_add_span(lambda h: q_ref[:, heads[h]], lambda h: kn_ref[:, heads[h]],
                     lambda h: vn_ref[:, heads[h]], SB_HEADS, tq, mt2, _strict_lower(tq), True,
                     *scratch)

    for cp in copies(t):
        cp.wait()
    slot = t % SB_RING
    past_rows = lambda h: pl.ds(h, n_pos, stride=SB_HEADS)
    _sb_add_span(lambda h: q_ref[:, heads[h]],
                 lambda h: kbuf[slot, past_rows(h), :].astype(BF16),
                 lambda h: vbuf[slot, past_rows(h), :].astype(BF16), SB_HEADS, n_pos, mt2, None, False,
                 *scratch)

    @pl.when(c == pl.num_programs(1) - 1)
    def _():
        for h in range(SB_HEADS):
            o_ref[:, heads[h]] = acc_ref[h].astype(o_ref.dtype)


def _sb_sample(qb, kb, vb, past_k, past_v, layer, mt2, batch, tq, *, chunk):
    past = past_k.shape[2] // SB_HEADS
    n_chunks = past // chunk
    new = pl.BlockSpec((tq, SB_WIDTH), lambda b, c: (b, 0))
    hbm = pl.BlockSpec(memory_space=pl.ANY)
    ring = pltpu.VMEM((SB_RING, chunk * SB_HEADS, HEAD_DIM), F32)
    return pl.pallas_call(
        functools.partial(_sb_sample_kernel, layer=layer), grid=(batch, n_chunks),
        in_specs=[new, new, new, hbm, hbm, _resident((2 * SB_BLK, 2 * SB_BLK))],
        out_specs=new,
        out_shape=jax.ShapeDtypeStruct((batch * tq, SB_WIDTH), BF16),
        scratch_shapes=_sb_scratch(SB_HEADS, tq, chunk) + [ring, ring,
                                                            pltpu.SemaphoreType.DMA((2, SB_RING))],
        compiler_params=_cparams(2), name="sb_sample",
    )(qb, kb, vb, past_k, past_v, mt2)


def _rec_blocks(tc):
    return [(n, h) for n in range(tc // REC_BLK) for h in range(REC_HEADS)]


def _rec_rows(n):
    return slice(n * REC_BLK, (n + 1) * REC_BLK)


def _rec_cols(h):
    return slice(h * HEAD_DIM, (h + 1) * HEAD_DIM)


def _rec_prepare(q, k, v, log_g, lbd):
    hi, lo = _split_bf16(log_g)
    cum = _dot(lbd, hi) + _dot(lbd, lo)
    row = lax.broadcasted_iota(jnp.int32, (REC_BLK, REC_BLK), 0)
    col = lax.broadcasted_iota(jnp.int32, (REC_BLK, REC_BLK), 1)
    causal = col <= row
    half = REC_BLK // 2
    prep = {}
    for n, h in _rec_blocks(q.shape[0]):
        rs, cs = _rec_rows(n), _rec_cols(h)
        c = cum[rs, cs]
        mid = c[half - 1:half, :]
        last = c[REC_BLK - 1:REC_BLK, :]
        q1 = q[rs, cs] * jnp.exp(c - mid)
        k1 = k[rs, cs] * jnp.exp(mid - c)
        vb = v[rs, cs].astype(BF16)
        s = jnp.where(causal, _dot_nt(q1.astype(BF16), k1.astype(BF16)), 0.0)
        o_intra = _dot(s.astype(BF16), vb)
        qd = (q1 * jnp.exp(mid)).astype(BF16)
        ds = _dot_tn(vb, (k1 * jnp.exp(last - mid)).astype(BF16))
        prep[n, h] = (o_intra, qd, ds, jnp.exp(last))
    return prep


def _rec_chain(prep, gate, norm_g, st_ref, o_ref):
    for n, h in _rec_blocks(gate.shape[0]):
        rs, cs = _rec_rows(n), _rec_cols(h)
        o_intra, qd, ds, decay = prep[n, h]
        st = st_ref[h]
        o = o_intra + _dot_nt(qd, st.astype(BF16))
        st_ref[h] = st * decay + ds
        ms = jnp.mean(o * o, axis=-1, keepdims=True)
        o = o * lax.rsqrt(ms + LN_EPS) * norm_g
        g = gate[rs, cs]
        o_ref[rs, cs] = (o * (g * _sigmoid(g))).astype(o_ref.dtype)


def _rec_kernel(gq_ref, gk_ref, gv_ref, gg_ref, ga_ref, hf_ref, hq_ref, hi_ref, ho_ref,
                wa_ref, ba_ref, gla_ng_ref, lb_ref, hg_ng_ref, lbd_ref, s0g_ref, s0h_ref,
                og_ref, oh_ref, sg_ref, sh_ref, stg_ref, sth_ref):
    c = pl.program_id(1)

    @pl.when(c == 0)
    def _():
        for h in range(REC_HEADS):
            stg_ref[h] = s0g_ref[0, h].T
            sth_ref[h] = s0h_ref[0, h].T

    lbd = lbd_ref[...]
    pre = _dot(ga_ref[...].astype(BF16), wa_ref[...]) + ba_ref[...]
    log_g = _log_sigmoid(pre) * (1.0 / GLA_GATE_NORM)
    prep_g = _rec_prepare(gq_ref[...] * ATT_SCALE, gk_ref[...], gv_ref[...], log_g, lbd)

    lb = lb_ref[...]
    zf = hf_ref[...]
    a = jnp.log(lb)
    b = jnp.log(1.0 - lb) + _log_sigmoid(zf)
    log_f = jnp.maximum(a, b) + _log1p_exp(-jnp.abs(a - b))
    k_h = (1.0 - lb) * _sigmoid(-zf)
    hq = hq_ref[...]
    q_h = hq * _sigmoid(hq) * ATT_SCALE
    prep_h = _rec_prepare(q_h, k_h, hi_ref[...], log_f, lbd)

    _rec_chain(prep_g, gg_ref[...], gla_ng_ref[...], stg_ref, og_ref)
    _rec_chain(prep_h, ho_ref[...], hg_ng_ref[...], sth_ref, oh_ref)

    @pl.when(c == pl.num_programs(1) - 1)
    def _():
        for h in range(REC_HEADS):
            sg_ref[0, h] = stg_ref[h].T
            sh_ref[0, h] = sth_ref[h].T


def _rec(rec, wa, ba, gla_ng, lb, hg_ng, lbd, s_gla, s_hg, s_layer, batch, t_b, tc):
    nc = t_b // tc
    col = lambda c0: pl.BlockSpec((tc, REC_WIDTH), lambda b, c, c0=c0: (b * nc + c, c0 // REC_WIDTH))
    full = lambda shape: pl.BlockSpec(shape, lambda b, c: (0,) * len(shape))
    a_spec = pl.BlockSpec((tc, HEAD_DIM), lambda b, c: (b * nc + c, R_GA // HEAD_DIM))
    state_in = pl.BlockSpec((None, 1, REC_HEADS, HEAD_DIM, HEAD_DIM),
                            lambda b, c: (s_layer, b, 0, 0, 0))
    state_out = pl.BlockSpec((1, REC_HEADS, HEAD_DIM, HEAD_DIM), lambda b, c: (b, 0, 0, 0))
    o_spec = pl.BlockSpec((tc, REC_WIDTH), lambda b, c: (b * nc + c, 0))
    o_shape = jax.ShapeDtypeStruct((batch * t_b, REC_WIDTH), BF16)
    s_shape = jax.ShapeDtypeStruct((batch, REC_HEADS, HEAD_DIM, HEAD_DIM), F32)
    state_scratch = pltpu.VMEM((REC_HEADS, HEAD_DIM, HEAD_DIM), F32)
    return pl.pallas_call(
        _rec_kernel, grid=(batch, nc),
        in_specs=[col(R_GQ), col(R_GK), col(R_GV), col(R_GG), a_spec,
                  col(R_HF), col(R_HQ), col(R_HI), col(R_HO),
                  full((HEAD_DIM, REC_WIDTH)), full((1, REC_WIDTH)), full((1, HEAD_DIM)),
                  full((1, REC_WIDTH)), full((1, HEAD_DIM)), full((tc, tc)), state_in, state_in],
        out_specs=[o_spec, o_spec, state_out, state_out],
        out_shape=[o_shape, o_shape, s_shape, s_shape],
        scratch_shapes=[state_scratch, state_scratch],
        compiler_params=_cparams(2), name="rec",
    )(rec, rec, rec, rec, rec, rec, rec, rec, rec, wa, ba, gla_ng, lb, hg_ng, lbd, s_gla, s_hg)


def _mix_kernel(x_ref, ga_ref, gb_ref, gc_ref, oa_ref, ob_ref, oc_ref,
                wsb_ref, wgla_ref, whg_ref, wo_ref, g_ref, b_ref, o_ref):
    m = _sigmoid(ga_ref[...]) * _dot(oa_ref[...], wsb_ref[...])
    m = m + _sigmoid(gb_ref[...]) * _dot(ob_ref[...], wgla_ref[...])
    m = m + _sigmoid(gc_ref[...]) * _dot(oc_ref[...], whg_ref[...])
    y = ALPHA * x_ref[...] + _dot(m.astype(BF16), wo_ref[...])
    o_ref[...] = _ln_rows(y, g_ref[...], b_ref[...])


def _mix(x, gates, oa, ob, oc, wsb, wgla, whg, wo, g, b, *, tm):
    t, d = x.shape
    row = lambda w, j=0: pl.BlockSpec((tm, w), lambda i, j=j: (i, j))
    return pl.pallas_call(
        _mix_kernel, grid=(t // tm,),
        in_specs=[row(d), row(d, 0), row(d, 1), row(d, 2),
                  row(SB_WIDTH), row(REC_WIDTH), row(REC_WIDTH),
                  _resident((SB_WIDTH, d)), _resident((REC_WIDTH, d)), _resident((REC_WIDTH, d)),
                  _resident((d, d)), _resident((1, d)), _resident((1, d))],
        out_specs=row(d),
        out_shape=jax.ShapeDtypeStruct((t, d), F32),
        compiler_params=_cparams(1), name="mix",
    )(x, gates, gates, gates, oa, ob, oc, wsb, wgla, whg, wo, g.reshape(1, d), b.reshape(1, d))


def _xattn_kernel(x_ref, mk_ref, mv_ref, wq_ref, wo_ref, g_ref, b_ref, o_ref):
    x = x_ref[...]
    n_b = mk_ref.shape[0]
    rows_b = x.shape[0] // n_b
    q = _dot(x.astype(BF16), wq_ref[...]).astype(BF16)
    pairs = [(r, h) for r in range(n_b) for h in range(XA_HEADS)]
    rows = lambda r: slice(r * rows_b, (r + 1) * rows_b)
    cols = lambda h: slice(h * HEAD_DIM, (h + 1) * HEAD_DIM)
    if len(mk_ref.shape) == 4:
        mem_head = lambda ref, r, h: ref[r, :, h, :].astype(BF16)
    else:
        mem_head = lambda ref, r, h: ref[r, :, cols(h)].astype(BF16)
    scores = {(r, h): _dot_nt(q[rows(r), cols(h)], mem_head(mk_ref, r, h)) for r, h in pairs}
    probs = {}
    for p in pairs:
        s = scores[p] * ATT_SCALE
        e = jnp.exp(s - jnp.max(s, axis=-1, keepdims=True))
        probs[p] = (e / jnp.sum(e, axis=-1, keepdims=True)).astype(BF16)
    outs = {(r, h): _dot(probs[r, h], mem_head(mv_ref, r, h)).astype(BF16) for r, h in pairs}
    per_batch = [jnp.concatenate([outs[r, h] for h in range(XA_HEADS)], axis=-1)
                 for r in range(n_b)]
    o = per_batch[0] if n_b == 1 else jnp.concatenate(per_batch, axis=0)
    y = ALPHA * x + _dot(o, wo_ref[...])
    o_ref[...] = _ln_rows(y, g_ref[...], b_ref[...])


def _xattn(x, mem_k, mem_v, wq, wo, g, b, *, batch, tm, mem_layer=None):
    t, d = x.shape
    t_b = t // batch
    if tm <= t_b:
        n_b, nt = 1, t_b // tm
        grid = (batch, nt)
        row = pl.BlockSpec((tm, d), lambda bi, i: (bi * nt + i, 0))
    else:
        n_b = tm // t_b
        grid = (batch // n_b, 1)
        row = pl.BlockSpec((tm, d), lambda bi, i: (bi, 0))
    if mem_layer is None:
        mem = pl.BlockSpec((n_b,) + mem_k.shape[1:], lambda bi, i: (bi, 0, 0))
    else:
        mem = pl.BlockSpec((None, n_b) + mem_k.shape[2:], lambda bi, i: (mem_layer, bi, 0, 0, 0))
    return pl.pallas_call(
        _xattn_kernel, grid=grid,
        in_specs=[row, mem, mem, _resident((d, XA_WIDTH)), _resident((XA_WIDTH, d)),
                  _resident((1, d)), _resident((1, d))],
        out_specs=row,
        out_shape=jax.ShapeDtypeStruct((t, d), F32),
        compiler_params=_cparams(2), name="xattn",
    )(x, mem_k, mem_v, wq, wo, g.reshape(1, d), b.reshape(1, d))


def _ffn_kernel(x_ref, wg_ref, wu_ref, wd_ref, g_ref, b_ref, o_ref, ob_ref, xs_ref, acc_ref):
    f = pl.program_id(1)

    @pl.when(f == 0)
    def _():
        xs_ref[...] = x_ref[...].astype(BF16)
        acc_ref[...] = jnp.zeros_like(acc_ref)

    xs = xs_ref[...]
    gate = _dot(xs, wg_ref[...])
    up = _dot(xs, wu_ref[...])
    hidden = gate * _sigmoid(gate) * up
    acc_ref[...] += _dot(hidden.astype(BF16), wd_ref[...])

    @pl.when(f == pl.num_programs(1) - 1)
    def _():
        y = _ln_rows(ALPHA * x_ref[...] + acc_ref[...], g_ref[...], b_ref[...])
        o_ref[...] = y
        ob_ref[...] = y.astype(BF16)


def _ffn(x, w_in, w_out, g, b, *, tm, tf):
    t, d = x.shape
    nf = D_FF // tf
    row = pl.BlockSpec((tm, d), lambda i, f: (i, 0))
    vec = pl.BlockSpec((1, d), lambda i, f: (0, 0))
    return pl.pallas_call(
        _ffn_kernel, grid=(t // tm, nf),
        in_specs=[row,
                  pl.BlockSpec((d, tf), lambda i, f: (0, f)),
                  pl.BlockSpec((d, tf), lambda i, f: (0, nf + f)),
                  pl.BlockSpec((tf, d), lambda i, f: (f, 0)),
                  vec, vec],
        out_specs=[row, row],
        out_shape=[jax.ShapeDtypeStruct((t, d), F32), jax.ShapeDtypeStruct((t, d), BF16)],
        scratch_shapes=[pltpu.VMEM((tm, d), BF16), pltpu.VMEM((tm, d), F32)],
        compiler_params=_cparams(2), name="ffn",
    )(x, w_in, w_in, w_out, g.reshape(1, d), b.reshape(1, d))


def _tri_tables():
    j = lax.broadcasted_iota(jnp.int32, (SB_BLK, SB_BLK), 0)
    s = lax.broadcasted_iota(jnp.int32, (SB_BLK, SB_BLK), 1)
    mt = jnp.concatenate([(j > s).astype(BF16), jnp.ones((SB_BLK, SB_BLK), BF16)], axis=1)
    return jnp.concatenate([mt, mt], axis=0)


def _block_cumsum_table(tc):
    r = lax.broadcasted_iota(jnp.int32, (tc, tc), 0)
    c = lax.broadcasted_iota(jnp.int32, (tc, tc), 1)
    return ((c <= r) & (r // REC_BLK == c // REC_BLK)).astype(BF16)


def _layer(x, xb, lw, layer, mem_k, mem_v, s_gla, s_hg, s_layer, past, ln_in, k_all, v_all, depth,
           *, batch, t_b, cfg):
    tm = cfg["tm_proj"]
    if ln_in is not None:
        x, xb = _ln_in(x, ln_in[0], ln_in[1], tm=cfg["tm_ln"])
    gates = _mm(xb, lw["w_gates"], F32, tm=tm, tn=2048)
    rec = _mm(xb, lw["w_rec"], F32, tm=tm, tn=N_REC // 3)
    k_all, v_all, qb, kb, vb = _proj_qkv(xb, lw["w_sqkv"], k_all, v_all, layer, depth,
                                         tm=cfg["tm_kv"])
    if past is None:
        oa = _sb_prompt(qb, kb, vb, lw["mt2"], batch, t_b)
    else:
        oa = _sb_sample(qb, kb, vb, past[0], past[1], layer, lw["mt2"], batch, t_b,
                        chunk=cfg["sb_chunk"])
    tc = cfg["tc"]
    lbd = _block_cumsum_table(tc)
    ob, oc, s_gla_new, s_hg_new = _rec(rec, lw["wa"], lw["ba"], lw["gla_ng"], lw["lb"], lw["hg_ng"],
                                       lbd, s_gla, s_hg, s_layer, batch, t_b, tc)
    x = _mix(x, gates, oa, ob, oc, lw["w_up_sb"], lw["w_up_gla"], lw["w_up_hg"], lw["w_o"],
             lw["ln1_g"], lw["ln1_b"], tm=cfg["tm_mix"])
    x = _xattn(x, mem_k, mem_v, lw["w_xq"], lw["w_xo"], lw["ln2_g"], lw["ln2_b"],
               batch=batch, tm=cfg["tm_xa"], mem_layer=None if past is None else layer)
    x, xb = _ffn(x, lw["w_ffn_in"], lw["w_ffn_out"], lw["ln3_g"], lw["ln3_b"],
                 tm=cfg["tm_ffn"], tf=cfg["tf"])
    return x, xb, k_all, v_all, s_gla_new, s_hg_new


def _tile(n, pref):
    t = min(n, pref)
    while n % t:
        t //= 2
    return t


def kernel(x_prompt, x_sample, mem_prompt, cache_sb_k, cache_sb_v, cache_mem_k, cache_mem_v,
           state_gla, state_hgrn, ln_in_g, ln_in_b, w_in, gla_w_a, gla_b_a, gla_norm_g, hg_lower,
           hg_norm_g, w_up_sb, w_up_gla, w_up_hg, w_o, ln1_g, ln1_b, w_xq, w_xk, w_xv, w_xo,
           ln2_g, ln2_b, w_ffn_in, w_ffn_out, ln3_g, ln3_b):
    bp, seq, d = x_prompt.shape
    bs, tq, _ = x_sample.shape
    n_mem = mem_prompt.shape[1]
    past_len = cache_sb_k.shape[2]
    depth = w_in.shape[0]

    lb_all = jnp.cumsum(jax.nn.softmax(hg_lower.astype(F32), axis=0), axis=0)
    lb_all = lb_all - lb_all[0:1]
    mt2 = _tri_tables()
    wa_pad = jnp.zeros((depth, HEAD_DIM, REC_WIDTH), BF16).at[:, :GLA_RANK].set(gla_w_a.astype(BF16))
    ga_zero = jnp.zeros((d, GA_PAD - GLA_RANK), BF16)

    xp = x_prompt.reshape(bp * seq, d)
    xs = x_sample.reshape(bs * tq, d)
    xpb = xsb = None
    mem2d = mem_prompt.reshape(bp * n_mem, d)
    zero_state = jnp.zeros((1, bp, REC_HEADS, HEAD_DIM, HEAD_DIM), F32)
    kp_all = vp_all = ks_all = vs_all = None
    past = (cache_sb_k.reshape(depth, bs, past_len * SB_HEADS, HEAD_DIM),
            cache_sb_v.reshape(depth, bs, past_len * SB_HEADS, HEAD_DIM))

    cfg_p = dict(tm_proj=_tile(bp * seq, 1024), tm_kv=_tile(bp * seq, 512), tc=_tile(seq, 512),
                 tm_mix=_tile(bp * seq, 256), tm_xa=_tile(seq, 512),
                 tm_ffn=_tile(bp * seq, 512), tf=512, tm_ln=_tile(bp * seq, 256))
    cfg_s = dict(tm_proj=_tile(bs * tq, 1024), tm_kv=_tile(bs * tq, 512), tc=_tile(tq, 256),
                 tm_mix=_tile(bs * tq, 256), tm_xa=_tile(bs * tq, 256),
                 tm_ffn=_tile(bs * tq, 512), tf=512, tm_ln=_tile(bs * tq, 256),
                 sb_chunk=_tile(past_len, 1024))

    outs_p, outs_s = [], []
    for l in range(depth):
        w = w_in[l]
        lw = dict(
            w_gates=w[:, W_GATES:].astype(BF16), w_sqkv=w[:, W_SQ:W_REC_A].astype(BF16),
            w_rec=jnp.concatenate([w[:, W_REC_A:W_GA].astype(BF16), w[:, W_REC_H:W_GATES].astype(BF16),
                                   w[:, W_GA:W_REC_H].astype(BF16), ga_zero], axis=1),
            mt2=mt2, wa=wa_pad[l], ba=gla_b_a[l].reshape(1, REC_WIDTH),
            gla_ng=gla_norm_g[l].reshape(1, HEAD_DIM), lb=lb_all[l].reshape(1, REC_WIDTH),
            hg_ng=hg_norm_g[l].reshape(1, HEAD_DIM),
            w_up_sb=w_up_sb[l].astype(BF16), w_up_gla=w_up_gla[l].astype(BF16),
            w_up_hg=w_up_hg[l].astype(BF16), w_o=w_o[l].astype(BF16),
            ln1_g=ln1_g[l], ln1_b=ln1_b[l], w_xq=w_xq[l].astype(BF16), w_xo=w_xo[l].astype(BF16),
            ln2_g=ln2_g[l], ln2_b=ln2_b[l], w_ffn_in=w_ffn_in[l].astype(BF16),
            w_ffn_out=w_ffn_out[l].astype(BF16), ln3_g=ln3_g[l], ln3_b=ln3_b[l])
        ln_in = (ln_in_g, ln_in_b) if l == 0 else None
        tm_mem = _tile(bp * n_mem, 512)
        mem_k = _mm(mem2d, w_xk[l].astype(BF16), F32, tm=tm_mem, tn=XA_WIDTH)
        mem_v = _mm(mem2d, w_xv[l].astype(BF16), F32, tm=tm_mem, tn=XA_WIDTH)
        xp, xpb, kp_all, vp_all, gp, hp = _layer(
            xp, xpb, lw, l, mem_k.reshape(bp, n_mem, XA_WIDTH), mem_v.reshape(bp, n_mem, XA_WIDTH),
            zero_state, zero_state, 0, None, ln_in, kp_all, vp_all, depth,
            batch=bp, t_b=seq, cfg=cfg_p)
        xs, xsb, ks_all, vs_all, gs, hs = _layer(
            xs, xsb, lw, l, cache_mem_k, cache_mem_v,
            state_gla, state_hgrn, l, past, ln_in, ks_all, vs_all, depth,
            batch=bs, t_b=tq, cfg=cfg_s)
        outs_p.append((mem_k.reshape(bp, n_mem, XA_HEADS, HEAD_DIM),
                       mem_v.reshape(bp, n_mem, XA_HEADS, HEAD_DIM), gp, hp))
        outs_s.append((gs, hs))
    stk = lambda outs, i: jnp.stack([o[i] for o in outs], axis=0)
    kv_p = (depth, bp, seq, SB_HEADS, HEAD_DIM)
    kv_s = (depth, bs, tq, SB_HEADS, HEAD_DIM)
    return (xp.reshape(bp, seq, d), xs.reshape(bs, tq, d),
            kp_all.reshape(kv_p), vp_all.reshape(kv_p), stk(outs_p, 0), stk(outs_p, 1),
            stk(outs_p, 2), stk(outs_p, 3), ks_all.reshape(kv_s), vs_all.reshape(kv_s),
            stk(outs_s, 0), stk(outs_s, 1))
```
